```python
import math
import jax, jax.numpy as jnp
from jax import lax
import numpy as np

D_MODEL = 2048
BATCH = 4
SEQ = 2048
DEPTH = 2

GRID_W = 64
CTX_LEN = 256
EPS = 1e-6
F32 = jnp.float32
NEG_BIG = -1e30
MIN_FORGET = 1e-6

HG_HEADS = 6
HG_DK = 128
HG_DV = 128
HG_WIDTH = HG_HEADS * HG_DK
HG_CHUNK = 64

NA_HEADS = 4
NA_HD = 128
NA_WIDTH = NA_HEADS * NA_HD
NA_WIN_ROWS = 8
NA_WIN_COLS = 16

SSD_HEADS = 12
SSD_HD = 64
SSD_WIDTH = SSD_HEADS * SSD_HD
SSD_GROUPS = 4
SSD_STATE = 128
SSD_CONV = 4
SSD_CHUNK = 128
SSD_CONV_CH = SSD_WIDTH + 2 * SSD_GROUPS * SSD_STATE

ROPE_BASE = 10000.0
N_BRANCH = 3
N_MIX_COLS = 5 * HG_WIDTH + 3 * NA_WIDTH + SSD_WIDTH + SSD_CONV_CH + 2 * SSD_HEADS
N_IN = N_MIX_COLS + N_BRANCH * D_MODEL

N_EXPERTS = 32
TOP_K = 4
D_FF = 2048
SWIGLU_ALPHA = 1.702
SWIGLU_LIMIT = 7.0
MOE_BLOCK = 128

kernel_name = "hybrid_hgrn2_natten_ssd_moe_dit"


def rms_norm(x, w):
    xf = x.astype(F32)
    y = xf * lax.rsqrt(jnp.mean(xf * xf, axis=-1, keepdims=True) + EPS)
    return (y * w.astype(F32)).astype(x.dtype)


def modulate(x, w, shift, scale):
    return rms_norm(x, w) * (1.0 + scale) + shift


def split_in(y):
    sizes = (HG_WIDTH,) * 5 + (NA_WIDTH,) * 3 + (SSD_WIDTH, SSD_CONV_CH, SSD_HEADS, SSD_HEADS)
    return jnp.split(y, np.cumsum(sizes).tolist(), axis=-1)


def axial_rope_2d(t):
    length, n = t.shape[1], t.shape[-1]
    half = n // 2
    pos = jnp.arange(length)
    inv_freq = 1.0 / (ROPE_BASE ** (jnp.arange(0, half, 2, dtype=F32) / half))

    def rotate(u, p):
        ang = p.astype(F32)[:, None] * inv_freq[None, :]
        cos, sin = jnp.cos(ang)[None, :, None, :], jnp.sin(ang)[None, :, None, :]
        u1, u2 = jnp.split(u.astype(F32), 2, axis=-1)
        return jnp.concatenate([u1 * cos - u2 * sin, u2 * cos + u1 * sin], axis=-1)

    out = jnp.concatenate([rotate(t[..., :half], pos // GRID_W), rotate(t[..., half:], pos % GRID_W)], axis=-1)
    return out.astype(t.dtype)


def depthwise_conv_centred(x, w, b):
    k = w.shape[0]
    y = lax.conv_general_dilated(x, w[:, None, :].astype(x.dtype), window_strides=(1,),
                                 padding=[((k - 1) // 2, k // 2)],
                                 dimension_numbers=("NWC", "WIO", "NWC"),
                                 feature_group_count=x.shape[-1])
    return y + b


def to_chunks(t, chunk):
    b, l = t.shape[:2]
    return jnp.moveaxis(t.reshape(b, l // chunk, chunk, *t.shape[2:]), 1, 0)


def from_chunks(t):
    t = jnp.moveaxis(t, 0, 1)
    return t.reshape(t.shape[0], t.shape[1] * t.shape[2], *t.shape[3:])


def masked_decay(cum, causal):
    diff = cum[:, :, None] - cum[:, None, :]
    return jnp.where(causal, jnp.exp(jnp.where(causal, diff, 0.0)), 0.0)


def gla_chunk_scan(q, k, v, logf, s0, return_y, chunk=HG_CHUNK):
    causal = jnp.tril(jnp.ones((chunk, chunk), bool))[None, :, :, None, None]

    def step(s, inp):
        qc, kc, vc, gc = inp
        cum = jnp.cumsum(gc.astype(F32), axis=1)
        last = cum[:, -1]
        kd = kc * jnp.exp(last[:, None] - cum)
        s_new = s * jnp.exp(last)[..., None] + jnp.einsum("bshk,bshv->bhkv", kd, vc)
        if not return_y:
            return s_new, None
        decay = masked_decay(cum, causal)
        att = jnp.einsum("bthk,bshk,btshk->bhts", qc, kc, decay)
        y = jnp.einsum("bhts,bshv->bthv", att, vc) + jnp.einsum("bthk,bhkv->bthv", qc * jnp.exp(cum), s)
        return s_new, y

    s_fin, ys = lax.scan(step, s0, tuple(to_chunks(t, chunk) for t in (q, k, v, logf)))
    return (from_chunks(ys) if return_y else None), s_fin


def ssd_chunk_scan(x, dt, da, bm, cm, s0, return_y, chunk=SSD_CHUNK):
    b, _, h, p = x.shape
    g, n = bm.shape[2], bm.shape[3]
    hg = h // g
    causal = jnp.tril(jnp.ones((chunk, chunk), bool))[None, :, :, None, None]

    def step(s, inp):
        xc, dtc, dac, bc, cc = inp
        xc = xc.reshape(b, chunk, g, hg, p)
        dtc = dtc.reshape(b, chunk, g, hg)
        cum = jnp.cumsum(dac.astype(F32).reshape(b, chunk, g, hg), axis=1)
        last = cum[:, -1]
        sg = s.reshape(b, g, hg, p, n)
        w_end = dtc * jnp.exp(last[:, None] - cum)
        s_new = sg * jnp.exp(last)[..., None, None] + jnp.einsum("bsgn,bsgj,bsgjp->bgjpn", bc, w_end, xc)
        s_new = s_new.reshape(b, h, p, n)
        if not return_y:
            return s_new, None
        seg = masked_decay(cum, causal)
        cb = jnp.einsum("btgn,bsgn->bgts", cc, bc)
        y = jnp.einsum("bgts,btsgj,bsgj,bsgjp->btgjp", cb, seg, dtc, xc)
        y = y + jnp.einsum("btgn,bgjpn->btgjp", cc, sg) * jnp.exp(cum)[..., None]
        return s_new, y.reshape(b, chunk, h, p)

    s_fin, ys = lax.scan(step, s0, tuple(to_chunks(t, chunk) for t in (x, dt, da, bm, cm)))
    return (from_chunks(ys) if return_y else None), s_fin


def bidirectional_scan(scan_fn, lat_fwd, lat_bwd, ctx_fwd, ctx_bwd, s_zero, need_ctx):
    flip = lambda args: tuple(jnp.flip(a, axis=1) for a in args)
    yc_f, sc_f = scan_fn(*ctx_fwd, s_zero, need_ctx)
    yc_b, sc_b = scan_fn(*flip(ctx_bwd), s_zero, need_ctx)
    yl_f, _ = scan_fn(*lat_fwd, sc_f, True)
    yl_b, _ = scan_fn(*flip(lat_bwd), sc_b, True)
    y_lat = yl_f + jnp.flip(yl_b, axis=1)
    y_ctx = yc_f + jnp.flip(yc_b, axis=1) if need_ctx else None
    return y_lat, y_ctx


def hgrn2_mixer(parts_lat, parts_ctx, lb, norm_w, need_ctx):
    def prep(q, f_fwd, f_bwd, i):
        heads = lambda t: t.reshape(t.shape[0], t.shape[1], HG_HEADS, -1)
        q = heads(jax.nn.silu(q)) * (HG_DK ** -0.5)
        v = heads(i)
        dirs = []
        for f_raw, lbd in ((f_fwd, lb[0]), (f_bwd, lb[1])):
            f = lbd + (1.0 - lbd) * jax.nn.sigmoid(f_raw.astype(F32))
            logf = jnp.log(jnp.maximum(f, MIN_FORGET))
            dirs.append((q, heads((1.0 - f).astype(i.dtype)), v, heads(logf)))
        return dirs

    def finish(o, g):
        gh = g.reshape(o.shape)
        o = rms_norm(o, norm_w) * jax.nn.silu(gh.astype(F32))
        return o.reshape(o.shape[0], o.shape[1], HG_WIDTH).astype(g.dtype)

    lf, lbw = prep(*parts_lat[:4])
    cf, cbw = prep(*parts_ctx[:4])
    s_zero = jnp.zeros((parts_lat[0].shape[0], HG_HEADS, HG_DK, HG_DV), F32)
    y_l, y_c = bidirectional_scan(gla_chunk_scan, lf, lbw, cf, cbw, s_zero, need_ctx)
    y_lat = finish(y_l, parts_lat[4])
    y_ctx = finish(y_c, parts_ctx[4]) if need_ctx else None
    return y_lat, y_ctx


def neighbourhood_attention(q_lat, k_lat, v_lat, q_ctx, k_ctx, v_ctx, rpb, need_ctx):
    b, s, h, d = q_lat.shape
    rows = s // GRID_W
    wr = min(NA_WIN_ROWS, rows)
    scale = d ** -0.5
    qg = q_lat.reshape(b, rows, GRID_W, h, d)
    kg = k_lat.reshape(b, rows, GRID_W, h, d)
    vg = v_lat.reshape(b, rows, GRID_W, h, d)
    r = jnp.arange(rows)
    row_idx = jnp.clip(r - wr // 2, 0, rows - wr)[:, None] + jnp.arange(wr)[None, :]
    k_blk = kg[:, row_idx]
    v_blk = vg[:, row_idx]
    j = jnp.arange(GRID_W)
    c0 = jnp.clip(j - NA_WIN_COLS // 2, 0, GRID_W - NA_WIN_COLS)
    col_mask = (j[None, :] >= c0[:, None]) & (j[None, :] < c0[:, None] + NA_WIN_COLS)
    dc = jnp.clip(j[None, :] - j[:, None], -(NA_WIN_COLS - 1), NA_WIN_COLS - 1)
    dr = row_idx - r[:, None]
    bias = rpb[:, (dr + NA_WIN_ROWS - 1)[:, None, :, None], (dc + NA_WIN_COLS - 1)[None, :, None, :]]
    s_lat = jnp.einsum("brjhd,brwchd->bhrjwc", qg, k_blk).astype(F32) * scale + bias.astype(F32)
    s_lat = jnp.where(col_mask[None, None, None, :, None, :], s_lat, NEG_BIG)
    s_lat = s_lat.reshape(b, h, rows, GRID_W, wr * GRID_W)
    s_ctx = jnp.einsum("brjhd,bnhd->bhrjn", qg, k_ctx).astype(F32) * scale
    p = jax.nn.softmax(jnp.concatenate([s_lat, s_ctx], axis=-1), axis=-1).astype(v_lat.dtype)
    p_lat = p[..., :wr * GRID_W].reshape(b, h, rows, GRID_W, wr, GRID_W)
    p_ctx = p[..., wr * GRID_W:]
    o = jnp.einsum("bhrjwc,brwchd->brjhd", p_lat, v_blk) + jnp.einsum("bhrjn,bnhd->brjhd", p_ctx, v_ctx)
    o_lat = o.reshape(b, s, h * d)
    if not need_ctx:
        return o_lat, None
    n_ctx = q_ctx.shape[1]
    p_cc = jax.nn.softmax(jnp.einsum("bmhd,bnhd->bhmn", q_ctx, k_ctx).astype(F32) * scale, axis=-1)
    o_ctx = jnp.einsum("bhmn,bnhd->bmhd", p_cc.astype(v_ctx.dtype), v_ctx).reshape(b, n_ctx, h * d)
    return o_lat, o_ctx


def ssd_mixer(parts_lat, parts_ctx, conv_w, conv_b, dt_bias, a_log, d_skip, norm_w, need_ctx):
    def prep(z, xbc, dt_fwd, dt_bwd, rotary):
        b, l, _ = xbc.shape
        xbc = jax.nn.silu(depthwise_conv_centred(xbc, conv_w, conv_b))
        xs, bm, cm = jnp.split(xbc, [SSD_WIDTH, SSD_WIDTH + SSD_GROUPS * SSD_STATE], axis=-1)
        xs = xs.reshape(b, l, SSD_HEADS, SSD_HD)
        bm = bm.reshape(b, l, SSD_GROUPS, SSD_STATE)
        cm = cm.reshape(b, l, SSD_GROUPS, SSD_STATE)
        if rotary:
            bm, cm = axial_rope_2d(bm), axial_rope_2d(cm)
        dirs = []
        for k, dt_raw in enumerate((dt_fwd, dt_bwd)):
            dt = jax.nn.softplus(dt_raw.astype(F32) + dt_bias[k])
            dirs.append((xs, dt, -dt * jnp.exp(a_log[k].astype(F32)), bm, cm))
        return xs, dirs

    def finish(y, xs, z):
        b, l = z.shape[:2]
        y = (y + d_skip[:, None] * xs).reshape(b, l, SSD_WIDTH) * jax.nn.silu(z.astype(F32))
        y = rms_norm(y.reshape(b, l, SSD_GROUPS, SSD_WIDTH // SSD_GROUPS), norm_w.reshape(SSD_GROUPS, -1))
        return y.reshape(b, l, SSD_WIDTH).astype(z.dtype)

    xs_l, (fl, bl) = prep(*parts_lat, True)
    xs_c, (fc, bc) = prep(*parts_ctx, False)
    s_zero = jnp.zeros((xs_l.shape[0], SSD_HEADS, SSD_HD, SSD_STATE), F32)
    y_l, y_c = bidirectional_scan(ssd_chunk_scan, fl, bl, fc, bc, s_zero, need_ctx)
    y_lat = finish(y_l, xs_l, parts_lat[0])
    y_ctx = finish(y_c, xs_c, parts_ctx[0]) if need_ctx else None
    return y_lat, y_ctx


def clamped_swiglu(gate, up):
    gate = jnp.minimum(gate, SWIGLU_LIMIT)
    up = jnp.clip(up, -SWIGLU_LIMIT, SWIGLU_LIMIT)
    return gate * jax.nn.sigmoid(SWIGLU_ALPHA * gate) * (up + 1.0)


def moe_ffn(x, w_router, b_router, w_gate, b_gate, w_up, b_up, w_down, b_down):
    t, d = x.shape
    logits = (x @ w_router).astype(F32) + b_router.astype(F32)
    top_val, top_idx = lax.top_k(logits, TOP_K)
    top_w = jax.nn.softmax(top_val, axis=-1)
    n_assign = t * TOP_K
    exp_id = top_idx.reshape(-1)
    tok_id = jnp.arange(n_assign) // TOP_K
    order = jnp.argsort(exp_id)
    e_sorted = exp_id[order]
    counts = jnp.bincount(exp_id, length=N_EXPERTS)
    padded = (counts + MOE_BLOCK - 1) // MOE_BLOCK * MOE_BLOCK
    pad_end = jnp.cumsum(padded)
    pad_start = pad_end - padded
    sort_start = jnp.cumsum(counts) - counts
    dest = pad_start[e_sorted] + jnp.arange(n_assign) - sort_start[e_sorted]
    n_blocks = (n_assign + N_EXPERTS * (MOE_BLOCK - 1) + MOE_BLOCK - 1) // MOE_BLOCK
    cap = n_blocks * MOE_BLOCK
    slot_tok = jnp.full((cap,), t, jnp.int32).at[dest].set(tok_id[order])
    slot_w = jnp.zeros((cap,), F32).at[dest].set(top_w.reshape(-1)[order])
    block_exp = jnp.minimum(jnp.searchsorted(pad_end, jnp.arange(n_blocks) * MOE_BLOCK, side="right"), N_EXPERTS - 1)
    x_pad = jnp.concatenate([x, jnp.zeros((1, d), x.dtype)], axis=0)
    xb = x_pad[slot_tok].reshape(n_blocks, MOE_BLOCK, d)

    def expert_block(args):
        xblk, e = args
        gate = xblk @ w_gate[e] + b_gate[e]
        up = xblk @ w_up[e] + b_up[e]
        return clamped_swiglu(gate, up) @ w_down[e] + b_down[e]

    yb = lax.map(expert_block, (xb, block_exp))
    out = jnp.zeros((t + 1, d), F32).at[slot_tok].add(yb.reshape(cap, d).astype(F32) * slot_w[:, None])
    return out[:t].astype(x.dtype)


def trunk_layer(x, xc, c, c_ctx, lb, w_mod, b_mod, norm1_w, norm2_w, w_in, hg_norm_w, na_q_norm_w,
                na_k_norm_w, na_rpb, ssd_conv_w, ssd_conv_b, ssd_dt_bias, ssd_a_log, ssd_d, ssd_norm_w,
                w_branch_hg, w_branch_na, w_branch_ssd, w_out, moe_w_router, moe_b_router, moe_w_gate,
                moe_b_gate, moe_w_up, moe_b_up, moe_w_down, moe_b_down, need_ctx):
    b, s, d = x.shape
    n_ctx = xc.shape[1]
    mod = jax.nn.silu(c) @ w_mod + b_mod
    mod_c = jax.nn.silu(c_ctx) @ w_mod + b_mod
    sh1, sc1, g1, sh2, sc2, g2 = jnp.split(mod[:, None, :], 6, axis=-1)
    csh1, csc1, cg1, csh2, csc2, cg2 = jnp.split(mod_c, 6, axis=-1)

    h = modulate(x, norm1_w, sh1, sc1)
    hc = modulate(xc, norm1_w, csh1, csc1)
    pl = split_in(h @ w_in)
    pc = split_in(hc @ (w_in if need_ctx else w_in[:, :N_MIX_COLS]))

    y_hg, yc_hg = hgrn2_mixer(pl[0:5], pc[0:5], lb, hg_norm_w, need_ctx)

    def na_qkv(parts):
        q, k, v = (t.reshape(t.shape[0], t.shape[1], NA_HEADS, NA_HD) for t in parts)
        return rms_norm(q, na_q_norm_w), rms_norm(k, na_k_norm_w), v

    ql, kl, vl = na_qkv(pl[5:8])
    qc, kc, vc = na_qkv(pc[5:8])
    y_na, yc_na = neighbourhood_attention(ql, kl, vl, qc, kc, vc, na_rpb, need_ctx)

    y_ssd, yc_ssd = ssd_mixer(pl[8:12], pc[8:12], ssd_conv_w, ssd_conv_b, ssd_dt_bias, ssd_a_log,
                              ssd_d, ssd_norm_w, need_ctx)

    def merge(gate_logits, ya, yb, yc_):
        ga, gb, gc = jnp.split(jax.nn.sigmoid(gate_logits), N_BRANCH, axis=-1)
        m = ga * (ya @ w_branch_hg) + gb * (yb @ w_branch_na) + gc * (yc_ @ w_branch_ssd)
        return m @ w_out

    moe = lambda tokens: moe_ffn(tokens, moe_w_router, moe_b_router, moe_w_gate, moe_b_gate,
                                 moe_w_up, moe_b_up, moe_w_down, moe_b_down)

    x = x + g1 * merge(pl[12], y_hg, y_na, y_ssd)
    h2 = modulate(x, norm2_w, sh2, sc2).reshape(b * s, d)
    if need_ctx:
        xc = xc + cg1 * merge(pc[12], yc_hg, yc_na, yc_ssd)
        h2c = modulate(xc, norm2_w, csh2, csc2).reshape(b * n_ctx, d)
        f = moe(jnp.concatenate([h2, h2c], axis=0))
        x = x + g2 * f[:b * s].reshape(b, s, d)
        xc = xc + cg2 * f[b * s:].reshape(b, n_ctx, d)
        return x, xc
    x = x + g2 * moe(h2).reshape(b, s, d)
    return x, None


def setup_inputs(seed: int = 0) -> dict:
    key = jax.random.key(seed)
    keys = iter(jax.random.split(key, 40))

    def normal(shape, scale):
        return jax.random.normal(next(keys), shape, F32) * scale

    def gain(shape):
        return 1.0 + normal(shape, 0.02)

    L, D = DEPTH, D_MODEL
    dt0 = jnp.exp(jax.random.uniform(next(keys), (L, 2, SSD_HEADS), F32, math.log(1e-3), math.log(1e-1)))
    a0 = jax.random.uniform(next(keys), (L, 2, SSD_HEADS), F32, 1.0, 16.0)
    return {
        "x": normal((BATCH, SEQ, D), 1.0),
        "c": normal((BATCH, D), 1.0),
        "ctx": normal((BATCH, CTX_LEN, D), 1.0),
        "c_ctx": normal((D,), 1.0),
        "w_mod": normal((L, D, 6 * D), 0.5 * D ** -0.5),
        "b_mod": normal((L, 6 * D), 0.01),
        "norm1_w": gain((L, D)),
        "norm2_w": gain((L, D)),
        "w_in": normal((L, D, N_IN), D ** -0.5),
        "hg_lb_logits": normal((L, 2, HG_WIDTH), 0.1),
        "hg_norm_w": gain((L, HG_DV)),
        "na_q_norm_w": gain((L, NA_HD)),
        "na_k_norm_w": gain((L, NA_HD)),
        "na_rpb": normal((L, NA_HEADS, 2 * NA_WIN_ROWS - 1, 2 * NA_WIN_COLS - 1), 0.02),
        "ssd_conv_w": normal((L, SSD_CONV, SSD_CONV_CH), SSD_CONV ** -0.5),
        "ssd_conv_b": normal((L, SSD_CONV_CH), 0.01),
        "ssd_dt_bias": dt0 + jnp.log(-jnp.expm1(-dt0)),
        "ssd_a_log": jnp.log(a0),
        "ssd_d": gain((L, SSD_HEADS)),
        "ssd_norm_w": gain((L, SSD_WIDTH)),
        "w_branch_hg": normal((L, HG_WIDTH, D), HG_WIDTH ** -0.5),
        "w_branch_na": normal((L, NA_WIDTH, D), NA_WIDTH ** -0.5),
        "w_branch_ssd": normal((L, SSD_WIDTH, D), SSD_WIDTH ** -0.5),
        "w_out": normal((L, D, D), D ** -0.5),
        "moe_w_router": normal((L, D, N_EXPERTS), D ** -0.5),
        "moe_b_router": normal((L, N_EXPERTS), 0.01),
        "moe_w_gate": normal((L, N_EXPERTS, D, D_FF), D ** -0.5),
        "moe_b_gate": normal((L, N_EXPERTS, D_FF), 0.01),
        "moe_w_up": normal((L, N_EXPERTS, D, D_FF), D ** -0.5),
        "moe_b_up": normal((L, N_EXPERTS, D_FF), 0.01),
        "moe_w_down": normal((L, N_EXPERTS, D_FF, D), D_FF ** -0.5),
        "moe_b_down": normal((L, N_EXPERTS, D), 0.01),
    }


def reference(x, c, ctx, c_ctx, w_mod, b_mod, norm1_w, norm2_w, w_in, hg_lb_logits, hg_norm_w,
              na_q_norm_w, na_k_norm_w, na_rpb, ssd_conv_w, ssd_conv_b, ssd_dt_bias, ssd_a_log, ssd_d,
              ssd_norm_w, w_branch_hg, w_branch_na, w_branch_ssd, w_out, moe_w_router, moe_b_router,
              moe_w_gate, moe_b_gate, moe_w_up, moe_b_up, moe_w_down, moe_b_down):
    lb_soft = jax.nn.softmax(hg_lb_logits.astype(F32), axis=0)
    lower_bounds = jnp.cumsum(lb_soft, axis=0) - lb_soft[0]
    stacked = (w_mod, b_mod, norm1_w, norm2_w, w_in, hg_norm_w, na_q_norm_w, na_k_norm_w, na_rpb,
               ssd_conv_w, ssd_conv_b, ssd_dt_bias, ssd_a_log, ssd_d, ssd_norm_w, w_branch_hg,
               w_branch_na, w_branch_ssd, w_out, moe_w_router, moe_b_router, moe_w_gate, moe_b_gate,
               moe_w_up, moe_b_up, moe_w_down, moe_b_down)
    h_lat, h_ctx = x, ctx
    for i in range(DEPTH):
        h_lat, h_ctx = trunk_layer(h_lat, h_ctx, c, c_ctx, lower_bounds[i], *[p[i] for p in stacked],
                                   need_ctx=(i < DEPTH - 1))
    return h_lat
```

```python
import functools
import math

import numpy as np
import jax
import jax.numpy as jnp
from jax import lax
from jax.experimental import pallas as pl
from jax.experimental.pallas import tpu as pltpu

F32 = jnp.float32
BF16 = jnp.bfloat16

LANES = 128
SUBLANES = 8
VMEM_BYTES_V7X = 64 * 1024 * 1024

GRID_W = 64
EPS = 1e-6
NEG_BIG = -1e30
MIN_FORGET = 1e-6
HG_HEADS, HG_DK = 6, 128
HG_WIDTH = HG_HEADS * HG_DK
HG_CHUNK = 64
HG_SUB = 16
NA_HEADS, NA_HD = 4, 128
NA_WIDTH = NA_HEADS * NA_HD
NA_WIN_ROWS, NA_WIN_COLS = 8, 16
SSD_HEADS, SSD_HD = 12, 64
SSD_WIDTH = SSD_HEADS * SSD_HD
SSD_GROUPS, SSD_STATE, SSD_CONV = 4, 128, 4
SSD_CHUNK = 128
SSD_CONV_CH = SSD_WIDTH + 2 * SSD_GROUPS * SSD_STATE
ROPE_BASE = 10000.0
N_EXPERTS, TOP_K = 32, 4
SWIGLU_ALPHA, SWIGLU_LIMIT = 1.702, 7.0

C_HG = 0
C_NA = 5 * HG_WIDTH
C_Z = C_NA + 3 * NA_WIDTH
C_XBC = C_Z + SSD_WIDTH
C_DT = C_XBC + SSD_CONV_CH
N_MIX_COLS = C_DT + 2 * SSD_HEADS
N_PROJ = 8192

MOD_CHUNK = 256
MOE_RB = 256
MOE_KB = 6
MOE_TF = 512


def _cparams(sem, vmem_mb):
    return pltpu.CompilerParams(dimension_semantics=sem, vmem_limit_bytes=vmem_mb * 1024 * 1024)


def _dot(a, b):
    return jnp.dot(a, b, preferred_element_type=F32)


def _dot_nt(a, b):
    return lax.dot_general(a, b, (((1,), (1,)), ((), ())), preferred_element_type=F32)


def _dot_tn(a, b):
    return lax.dot_general(a, b, (((0,), (0,)), ((), ())), preferred_element_type=F32)


def _sigmoid(x):
    return 1.0 / (1.0 + jnp.exp(-x))


def _silu(x):
    return x * _sigmoid(x)


def _split3(x):
    a = x.astype(BF16)
    r = x - a.astype(F32)
    b = r.astype(BF16)
    c = (r - b.astype(F32)).astype(BF16)
    return a, b, c


def _tri_cumsum(tri, g):
    a, b, c = _split3(g)
    return _dot(tri, a) + _dot(tri, b) + _dot(tri, c)


def _scan_chunk(step, nchunks, nctx, reverse):
    if not reverse:
        return step
    return jnp.where(step < nctx, nctx - 1 - step, nchunks - 1 - (step - nctx))


def _tri_matrix(n, reverse):
    t = lax.broadcasted_iota(jnp.int32, (n, n), 0)
    s = lax.broadcasted_iota(jnp.int32, (n, n), 1)
    keep = (s >= t) if reverse else (s <= t)
    return keep, jnp.where(keep, 1.0, 0.0).astype(BF16)


def _mod_kernel(c_ref, w_ref, b_ref, o_ref):
    a = _silu(c_ref[...]).astype(BF16)
    o_ref[...] = _dot(a, w_ref[...].astype(BF16)) + b_ref[...]


def _mod_table(cvecs, w_mod, b_mod):
    depth, d, n = w_mod.shape
    tn = 1024
    return pl.pallas_call(
        _mod_kernel,
        out_shape=jax.ShapeDtypeStruct((depth, SUBLANES, n), F32),
        grid=(depth, n // tn),
        in_specs=[
            pl.BlockSpec((SUBLANES, d), lambda l, j: (0, 0)),
            pl.BlockSpec((None, d, tn), lambda l, j: (l, 0, j)),
            pl.BlockSpec((None, 1, tn), lambda l, j: (l, 0, j)),
        ],
        out_specs=pl.BlockSpec((None, SUBLANES, tn), lambda l, j: (l, 0, j)),
        compiler_params=_cparams(("arbitrary", "arbitrary"), 40),
        name="mod_table",
    )(cvecs, w_mod, b_mod.reshape(depth, 1, n))


def _mod_row_index(chunk, chunks_per_seq, ctx_chunks, batch):
    within = chunk % chunks_per_seq
    return jnp.where(within < ctx_chunks, batch, chunk // chunks_per_seq)


def _rms(x, w):
    return x * lax.rsqrt(jnp.mean(x * x, axis=-1, keepdims=True) + EPS) * w


def _inproj_kernel(x_ref, mod_ref, nw_ref, w_ref, y_ref, h_ref, *, tm, d, chunks_per_seq, ctx_chunks, batch):
    i, j = pl.program_id(0), pl.program_id(1)

    @pl.when(j == 0)
    def _():
        for s in range(tm // MOD_CHUNK):
            row = _mod_row_index(i * (tm // MOD_CHUNK) + s, chunks_per_seq, ctx_chunks, batch)
            shift = mod_ref[pl.ds(row, 1), 0:d]
            scale = mod_ref[pl.ds(row, 1), d:2 * d]
            xs = x_ref[s * MOD_CHUNK:(s + 1) * MOD_CHUNK, :]
            h = _rms(xs, nw_ref[...]) * (1.0 + scale) + shift
            h_ref[s * MOD_CHUNK:(s + 1) * MOD_CHUNK, :] = h.astype(BF16)

    y_ref[...] = _dot(h_ref[...], w_ref[...])


def _inproj(x2d, mod_l, norm_w, w_in_bf, layer, *, chunks_per_seq, ctx_chunks, batch):
    r, d = x2d.shape
    tm, tn = 512, 512
    kern = functools.partial(_inproj_kernel, tm=tm, d=d, chunks_per_seq=chunks_per_seq,
                             ctx_chunks=ctx_chunks, batch=batch)
    return pl.pallas_call(
        kern,
        out_shape=(jax.ShapeDtypeStruct((r, N_PROJ), F32), jax.ShapeDtypeStruct((r, d), BF16)),
        grid=(r // tm, N_PROJ // tn),
        in_specs=[
            pl.BlockSpec((tm, d), lambda i, j: (i, 0)),
            pl.BlockSpec(mod_l.shape, lambda i, j: (0, 0)),
            pl.BlockSpec((None, 1, d), lambda i, j: (layer, 0, 0)),
            pl.BlockSpec((None, d, tn), lambda i, j: (layer, 0, j)),
        ],
        out_specs=(pl.BlockSpec((tm, tn), lambda i, j: (i, j)),
                   pl.BlockSpec((tm, d), lambda i, j: (i, 0))),
        compiler_params=_cparams(("arbitrary", "arbitrary"), 40),
        name="inproj",
    )(x2d, mod_l, norm_w, w_in_bf)


def _hg_chunk(q, k, v, cum, st, reverse):
    c = q.shape[0]
    nb = c // HG_SUB
    tot = cum[0:1] if reverse else cum[c - 1:c]
    qe = (q * jnp.exp(cum)).astype(BF16)
    o_inter = _dot_nt(qe, st.astype(BF16))
    kd = k * jnp.exp(tot - cum)
    st_new = st * jnp.exp(tot) + _dot_tn(v.astype(BF16), kd.astype(BF16))

    t3 = lax.broadcasted_iota(jnp.int32, (HG_SUB, HG_SUB, 1), 0)
    s3 = lax.broadcasted_iota(jnp.int32, (HG_SUB, HG_SUB, 1), 1)
    keep3 = (s3 >= t3) if reverse else (s3 <= t3)
    outs = []
    for i in range(nb):
        lo, hi = i * HG_SUB, (i + 1) * HG_SUB
        qi, ki, vi, ci = q[lo:hi], k[lo:hi], v[lo:hi], cum[lo:hi]
        diff = ci[:, None, :] - ci[None, :, :]
        dec = jnp.where(keep3, jnp.exp(jnp.where(keep3, diff, 0.0)), 0.0)
        a3 = jnp.sum(qi[:, None, :] * ki[None, :, :] * dec, axis=-1, keepdims=True)
        oi = jnp.sum(a3 * vi[None, :, :], axis=1)
        if reverse and i < nb - 1:
            cs = cum[hi:hi + 1]
            kk, vv, cc = k[hi:], v[hi:], cum[hi:]
        elif (not reverse) and i > 0:
            cs = cum[lo - 1:lo]
            kk, vv, cc = k[:lo], v[:lo], cum[:lo]
        else:
            cs = None
        if cs is not None:
            qt = (qi * jnp.exp(ci - cs)).astype(BF16)
            kh = (kk * jnp.exp(cs - cc)).astype(BF16)
            a = _dot_nt(qt, kh)
            oi = oi + _dot(a.astype(BF16), vv.astype(BF16))
        outs.append(oi)
    return jnp.concatenate(outs, axis=0) + o_inter, st_new


def _hgrn2_kernel(q_ref, ff_ref, fb_ref, i_ref, g_ref, lb_ref, nw_ref, o_ref, acc_ref, *, ctx, lt):
    nchunks = lt // HG_CHUNK
    nctx = ctx // HG_CHUNK
    for direction, f_ref in enumerate((ff_ref, fb_ref)):
        reverse = direction == 1
        _, tri = _tri_matrix(HG_CHUNK, reverse)
        lbd = lb_ref[direction:direction + 1, :]

        def body(cidx, st, f_ref=f_ref, reverse=reverse, tri=tri, lbd=lbd, direction=direction):
            ci = _scan_chunk(cidx, nchunks, nctx, reverse)
            rows = pl.ds(pl.multiple_of(ci * HG_CHUNK, HG_CHUNK), HG_CHUNK)
            q = _silu(q_ref[rows, :]) * (HG_DK ** -0.5)
            f = lbd + (1.0 - lbd) * _sigmoid(f_ref[rows, :])
            logf = jnp.log(jnp.maximum(f, MIN_FORGET))
            k = 1.0 - f
            v = i_ref[rows, :]
            cum = _tri_cumsum(tri, logf)
            o, st_new = _hg_chunk(q, k, v, cum, st, reverse)
            if direction == 0:
                acc_ref[rows, :] = o
            else:
                acc_ref[rows, :] += o
            return st_new

        lax.fori_loop(0, nchunks, body, jnp.zeros((HG_DK, HG_DK), F32))

    y = acc_ref[...]
    o_ref[...] = (_rms(y, nw_ref[...]) * _silu(g_ref[...])).astype(o_ref.dtype)


def _hgrn2(y3, lb, norm_w, layer, ctx):
    b, lt, _ = y3.shape
    blk = lambda off: pl.BlockSpec((None, lt, HG_DK), lambda bi, h, off=off: (bi, 0, off + h))
    return pl.pallas_call(
        functools.partial(_hgrn2_kernel, ctx=ctx, lt=lt),
        out_shape=jax.ShapeDtypeStruct((b, lt, HG_WIDTH), BF16),
        grid=(b, HG_HEADS),
        in_specs=[blk(0), blk(HG_HEADS), blk(2 * HG_HEADS), blk(3 * HG_HEADS), blk(4 * HG_HEADS),
                  pl.BlockSpec((2, HG_DK), lambda bi, h: (0, h)),
                  pl.BlockSpec((None, 1, HG_DK), lambda bi, h: (layer, 0, 0))],
        out_specs=pl.BlockSpec((None, lt, HG_DK), lambda bi, h: (bi, 0, h)),
        scratch_shapes=[pltpu.VMEM((lt, HG_DK), F32)],
        compiler_params=_cparams(("arbitrary", "arbitrary"), 40),
        name="hgrn2",
    )(y3, y3, y3, y3, y3, lb, norm_w)


def _na_kernel(q_ref, k_ref, v_ref, bias_ref, qw_ref, kw_ref, o_ref, qs_ref, ks_ref, vs_ref, *, ctx, seq):
    rows = seq // GRID_W
    wr = min(NA_WIN_ROWS, rows)
    scale = NA_HD ** -0.5
    qs_ref[...] = (_rms(q_ref[...], qw_ref[...]) * scale).astype(BF16)
    ks_ref[...] = _rms(k_ref[...], kw_ref[...]).astype(BF16)
    vs_ref[...] = v_ref[...].astype(BF16)
    kc = ks_ref[0:ctx, :]
    vc = vs_ref[0:ctx, :]

    s = _dot_nt(qs_ref[0:ctx, :], kc)
    p = jnp.exp(s - jnp.max(s, axis=-1, keepdims=True))
    o = _dot(p.astype(BF16), vc) / jnp.sum(p, axis=-1, keepdims=True)
    o_ref[0:ctx, :] = o.astype(o_ref.dtype)

    def body(r, carry):
        r0 = jnp.clip(r - wr // 2, 0, rows - wr)
        qrow = pl.ds(pl.multiple_of(ctx + r * GRID_W, GRID_W), GRID_W)
        krow = pl.ds(pl.multiple_of(ctx + r0 * GRID_W, GRID_W), wr * GRID_W)
        qr = qs_ref[qrow, :]
        s_lat = _dot_nt(qr, ks_ref[krow, :]) + bias_ref[r0 - r + NA_WIN_ROWS - 1]
        s_ctx = _dot_nt(qr, kc)
        m = jnp.maximum(jnp.max(s_lat, axis=-1, keepdims=True), jnp.max(s_ctx, axis=-1, keepdims=True))
        p_lat = jnp.exp(s_lat - m)
        p_ctx = jnp.exp(s_ctx - m)
        den = jnp.sum(p_lat, axis=-1, keepdims=True) + jnp.sum(p_ctx, axis=-1, keepdims=True)
        o = (_dot(p_lat.astype(BF16), vs_ref[krow, :]) + _dot(p_ctx.astype(BF16), vc)) / den
        o_ref[qrow, :] = o.astype(o_ref.dtype)
        return carry

    lax.fori_loop(0, rows, body, 0)


def _na_bias_table(rpb, rows):
    wr = min(NA_WIN_ROWS, rows)
    j = np.arange(GRID_W)
    c0 = np.clip(j - NA_WIN_COLS // 2, 0, GRID_W - NA_WIN_COLS)
    col_mask = (j[None, :] >= c0[:, None]) & (j[None, :] < c0[:, None] + NA_WIN_COLS)
    dc = np.clip(j[None, :] - j[:, None], -(NA_WIN_COLS - 1), NA_WIN_COLS - 1) + NA_WIN_COLS - 1
    ri = np.arange(NA_WIN_ROWS)[:, None] + np.arange(wr)[None, :]
    t = rpb[:, ri[:, None, :, None], dc[None, :, None, :]]
    t = jnp.where(col_mask[None, None, :, None, :], t.astype(F32), NEG_BIG)
    return t.reshape(rpb.shape[0], NA_WIN_ROWS, GRID_W, wr * GRID_W)


def _nattn(y3, bias_tab, qw, kw, layer, ctx):
    b, lt, _ = y3.shape
    seq = lt - ctx
    base = C_NA // NA_HD
    blk = lambda off: pl.BlockSpec((None, lt, NA_HD), lambda bi, h, off=off: (bi, 0, base + off + h))
    return pl.pallas_call(
        functools.partial(_na_kernel, ctx=ctx, seq=seq),
        out_shape=jax.ShapeDtypeStruct((b, lt, NA_WIDTH), BF16),
        grid=(b, NA_HEADS),
        in_specs=[blk(0), blk(NA_HEADS), blk(2 * NA_HEADS),
                  pl.BlockSpec((None,) + bias_tab.shape[1:], lambda bi, h: (h, 0, 0, 0)),
                  pl.BlockSpec((None, 1, NA_HD), lambda bi, h: (layer, 0, 0)),
                  pl.BlockSpec((None, 1, NA_HD), lambda bi, h: (layer, 0, 0))],
        out_specs=pl.BlockSpec((None, lt, NA_HD), lambda bi, h: (bi, 0, h)),
        scratch_shapes=[pltpu.VMEM((lt, NA_HD), BF16)] * 3,
        compiler_params=_cparams(("arbitrary", "arbitrary"), 40),
        name="nattn",
    )(y3, y3, y3, bias_tab, qw, kw)


def _rope_tables(seq):
    half = SSD_STATE // 2
    pos = np.arange(seq)
    inv_freq = 1.0 / (ROPE_BASE ** (np.arange(0, half, 2, dtype=np.float64) / half))
    lane = np.arange(SSD_STATE)
    p = np.where(lane[None, :] < half, (pos // GRID_W)[:, None], (pos % GRID_W)[:, None])
    ang = p * inv_freq[lane % (half // 2)][None, :]
    sign = np.where((lane % half) < half // 2, -1.0, 1.0)
    return jnp.asarray(np.cos(ang), F32), jnp.asarray(np.sin(ang) * sign[None, :], F32)


def _conv_silu(x, w, bias, ctx, lt):
    t = lax.broadcasted_iota(jnp.int32, (lt, 1), 0)
    pos = jnp.where(t < ctx, t, t - ctx)
    length = jnp.where(t < ctx, ctx, lt - ctx)
    left = (SSD_CONV - 1) // 2
    acc = bias + jnp.zeros_like(x)
    for j in range(SSD_CONV):
        off = j - left
        xs = x if off == 0 else pltpu.roll(x, (-off) % lt, axis=0)
        ok = (pos + off >= 0) & (pos + off < length)
        acc = acc + jnp.where(ok, xs, 0.0) * w[j:j + 1, :]
    return _silu(acc)


def _rope(t, cos, sin_signed):
    lane = lax.broadcasted_iota(jnp.int32, t.shape, 1)
    quarter = SSD_STATE // 4
    swapped = jnp.where((lane & quarter) == 0, pltpu.roll(t, SSD_STATE - quarter, axis=1),
                        pltpu.roll(t, quarter, axis=1))
    return t * cos + swapped * sin_signed


def _ssd_kernel(x_ref, b0_ref, b1_ref, c0_ref, c1_ref, dt_ref, wx_ref, wb0_ref, wb1_ref, wc0_ref, wc1_ref,
                bx_ref, bb0_ref, bb1_ref, bc0_ref, bc1_ref, dtb_ref, alog_ref, dskip_ref, cos_ref, sin_ref,
                o_ref, xs_ref, bs0_ref, bs1_ref, cs0_ref, cs1_ref, dts_ref, *, ctx, lt):
    nchunks = lt // SSD_CHUNK
    nctx = ctx // SSD_CHUNK
    pair = pl.program_id(1)

    xs_ref[...] = _conv_silu(x_ref[...], wx_ref[...], bx_ref[...], ctx, lt)
    for src, w, bb, dst in ((b0_ref, wb0_ref, bb0_ref, bs0_ref), (b1_ref, wb1_ref, bb1_ref, bs1_ref),
                            (c0_ref, wc0_ref, bc0_ref, cs0_ref), (c1_ref, wc1_ref, bc1_ref, cs1_ref)):
        t = _conv_silu(src[...], w[...], bb[...], ctx, lt)
        dst[0:ctx, :] = t[0:ctx].astype(BF16)
        dst[ctx:lt, :] = _rope(t[ctx:lt], cos_ref[...], sin_ref[...]).astype(BF16)
    raw = dt_ref[...] + dtb_ref[...]
    dts_ref[...] = jnp.maximum(raw, 0.0) + jnp.log(1.0 + jnp.exp(-jnp.abs(raw)))

    lane = lax.broadcasted_iota(jnp.int32, (SSD_CHUNK, LANES), 1)
    a_neg = -jnp.exp(alog_ref[...])
    for direction in range(2):
        reverse = direction == 1
        keep, tri = _tri_matrix(SSD_CHUNK, reverse)

        def body(cidx, states, reverse=reverse, keep=keep, tri=tri, direction=direction):
            ci = _scan_chunk(cidx, nchunks, nctx, reverse)
            rows = pl.ds(pl.multiple_of(ci * SSD_CHUNK, SSD_CHUNK), SSD_CHUNK)
            x = xs_ref[rows, :]
            dt_all = dts_ref[rows, :]
            da_all = dt_all * a_neg
            y = jnp.zeros((SSD_CHUNK, LANES), F32)
            new_states = []
            for hh, (bs_ref, cs_ref) in enumerate(((bs0_ref, cs0_ref), (bs1_ref, cs1_ref))):
                chan = direction * SSD_HEADS + 2 * pair + hh
                sel = lane == chan
                dt_c = jnp.sum(jnp.where(sel, dt_all, 0.0), axis=-1, keepdims=True)
                da_c = jnp.sum(jnp.where(sel, da_all, 0.0), axis=-1, keepdims=True)
                dt_cb = jnp.broadcast_to(dt_c, (SSD_CHUNK, LANES))
                cum_cb = _tri_cumsum(tri, jnp.broadcast_to(da_c, (SSD_CHUNK, LANES)))
                cum_rb = cum_cb.T
                dt_rb = dt_cb.T
                tot = cum_cb[0:1] if reverse else cum_cb[SSD_CHUNK - 1:SSD_CHUNK]
                seg = jnp.where(keep, jnp.exp(jnp.where(keep, cum_cb - cum_rb, 0.0)), 0.0)
                bm = bs_ref[rows, :]
                cm = cs_ref[rows, :]
                m = _dot_nt(cm, bm) * seg * dt_rb
                half_mask = (lane // SSD_HD) == hh
                xm = jnp.where(half_mask, x, 0.0)
                st = states[hh]
                yh = _dot(m.astype(BF16), xm.astype(BF16)) + _dot(cm, st.astype(BF16)) * jnp.exp(cum_cb)
                w_end = dt_cb * jnp.exp(tot - cum_cb)
                st_new = st * jnp.exp(tot) + _dot_tn(bm, (xm * w_end).astype(BF16))
                y = y + yh
                new_states.append(st_new)
            if direction == 0:
                o_ref[rows, :] = y + dskip_ref[...] * x
            else:
                o_ref[rows, :] += y
            return tuple(new_states)

        zero = jnp.zeros((SSD_STATE, LANES), F32)
        lax.fori_loop(0, nchunks, body, (zero, zero))


def _ssd(y3, conv_w, conv_b, dt_bias_l, a_log_l, d_lanes, cos_t, sin_t, layer, ctx):
    b, lt, _ = y3.shape
    npair = SSD_HEADS // 2
    hg = SSD_HEADS // SSD_GROUPS
    xb = C_XBC // LANES
    bb = xb + SSD_WIDTH // LANES
    cb = bb + SSD_GROUPS
    g0 = lambda p: (2 * p) // hg
    g1 = lambda p: (2 * p + 1) // hg
    yblk = lambda f: pl.BlockSpec((None, lt, LANES), lambda bi, p, f=f: (bi, 0, f(p)))
    wblk = lambda f: pl.BlockSpec((None, SSD_CONV, LANES), lambda bi, p, f=f: (layer, 0, f(p)))
    bblk = lambda f: pl.BlockSpec((None, 1, LANES), lambda bi, p, f=f: (layer, 0, f(p)))
    cols = [lambda p: p, lambda p: npair + g0(p), lambda p: npair + g1(p),
            lambda p: npair + SSD_GROUPS + g0(p), lambda p: npair + SSD_GROUPS + g1(p)]
    const2 = lambda shape: pl.BlockSpec(shape, lambda bi, p: (0, 0))
    return pl.pallas_call(
        functools.partial(_ssd_kernel, ctx=ctx, lt=lt),
        out_shape=jax.ShapeDtypeStruct((b, lt, SSD_WIDTH), F32),
        grid=(b, npair),
        in_specs=[yblk(lambda p: xb + p), yblk(lambda p: bb + g0(p)), yblk(lambda p: bb + g1(p)),
                  yblk(lambda p: cb + g0(p)), yblk(lambda p: cb + g1(p)), yblk(lambda p: C_DT // LANES)]
                 + [wblk(f) for f in cols] + [bblk(f) for f in cols]
                 + [const2((1, LANES)), const2((1, LANES)),
                    pl.BlockSpec((None, 1, LANES), lambda bi, p: (layer, 0, p)),
                    const2(cos_t.shape), const2(sin_t.shape)],
        out_specs=pl.BlockSpec((None, lt, LANES), lambda bi, p: (bi, 0, p)),
        scratch_shapes=[pltpu.VMEM((lt, LANES), F32)] + [pltpu.VMEM((lt, LANES), BF16)] * 4
                       + [pltpu.VMEM((lt, LANES), F32)],
        compiler_params=_cparams(("arbitrary", "arbitrary"), 48),
        name="ssd",
    )(y3, y3, y3, y3, y3, y3, *([conv_w] * 5), *([conv_b] * 5), dt_bias_l, a_log_l, d_lanes, cos_t, sin_t)


def _mix_kernel(h_ref, yh_ref, yn_ref, ys_ref, z_ref, snw_ref, wh_ref, wn_ref, ws_ref, g0_ref, g1_ref, g2_ref,
                m_ref, ysn_ref):
    @pl.when(pl.program_id(1) == 0)
    def _():
        y = ys_ref[...] * _silu(z_ref[...])
        gw = SSD_WIDTH // SSD_GROUPS
        lane = lax.broadcasted_iota(jnp.int32, y.shape, 1)
        inv = jnp.zeros_like(y)
        for g in range(SSD_GROUPS):
            in_g = (lane >= g * gw) & (lane < (g + 1) * gw)
            ms = jnp.sum(jnp.where(in_g, y * y, 0.0), axis=-1, keepdims=True) * (1.0 / gw)
            inv = jnp.where(in_g, lax.rsqrt(ms + EPS), inv)
        ysn_ref[...] = (y * inv * snw_ref[...]).astype(BF16)

    h = h_ref[...]
    m = _sigmoid(_dot(h, g0_ref[...])) * _dot(yh_ref[...], wh_ref[...])
    m = m + _sigmoid(_dot(h, g1_ref[...])) * _dot(yn_ref[...], wn_ref[...])
    m = m + _sigmoid(_dot(h, g2_ref[...])) * _dot(ysn_ref[...], ws_ref[...])
    m_ref[...] = m.astype(m_ref.dtype)


def _mix(h, y_hg, y_na, y_ssd, y2d, ssd_nw, wb_hg, wb_na, wb_ssd, w_gate_bf, layer):
    r, d = h.shape
    tm, tn = 512, 512
    nt = d // tn
    row = lambda w: pl.BlockSpec((tm, w), lambda i, j: (i, 0))
    wspec = lambda k: pl.BlockSpec((None, k, tn), lambda i, j: (layer, 0, j))
    gspec = lambda br: pl.BlockSpec((None, d, tn), lambda i, j, br=br: (layer, 0, br * nt + j))
    return pl.pallas_call(
        _mix_kernel,
        out_shape=jax.ShapeDtypeStruct((r, d), BF16),
        grid=(r // tm, nt),
        in_specs=[row(d), row(HG_WIDTH), row(NA_WIDTH), row(SSD_WIDTH),
                  pl.BlockSpec((tm, SSD_WIDTH), lambda i, j: (i, C_Z // SSD_WIDTH)),
                  pl.BlockSpec((None, 1, SSD_WIDTH), lambda i, j: (layer, 0, 0)),
                  wspec(HG_WIDTH), wspec(NA_WIDTH), wspec(SSD_WIDTH), gspec(0), gspec(1), gspec(2)],
        out_specs=pl.BlockSpec((tm, tn), lambda i, j: (i, j)),
        scratch_shapes=[pltpu.VMEM((tm, SSD_WIDTH), BF16)],
        compiler_params=_cparams(("arbitrary", "arbitrary"), 48),
        name="mix",
    )(h, y_hg, y_na, y_ssd, y2d, ssd_nw, wb_hg, wb_na, wb_ssd, w_gate_bf, w_gate_bf, w_gate_bf)


def _outproj_kernel(x_ref, m_ref, mod_ref, nw_ref, wo_ref, wr_ref, br_ref, xo_ref, h2_ref, lg_ref,
                    *, d, chunks_per_seq, ctx_chunks, batch):
    row = _mod_row_index(pl.program_id(0), chunks_per_seq, ctx_chunks, batch)
    g1 = mod_ref[pl.ds(row, 1), 2 * d:3 * d]
    sh2 = mod_ref[pl.ds(row, 1), 3 * d:4 * d]
    sc2 = mod_ref[pl.ds(row, 1), 4 * d:5 * d]
    x = x_ref[...] + g1 * _dot(m_ref[...], wo_ref[...])
    xo_ref[...] = x
    h2 = _rms(x, nw_ref[...]) * (1.0 + sc2) + sh2
    h2_ref[...] = h2.astype(BF16)
    a, b, _ = _split3(h2)
    wa, wb, _ = _split3(wr_ref[...])
    lg_ref[...] = _dot(a, wa) + _dot(a, wb) + _dot(b, wa) + br_ref[...]


def _outproj(x2d, m, mod_l, norm_w, w_out_bf, w_router, b_router, layer, *, chunks_per_seq, ctx_chunks, batch):
    r, d = x2d.shape
    tm = MOD_CHUNK
    kern = functools.partial(_outproj_kernel, d=d, chunks_per_seq=chunks_per_seq, ctx_chunks=ctx_chunks,
                             batch=batch)
    return pl.pallas_call(
        kern,
        out_shape=(jax.ShapeDtypeStruct((r, d), F32), jax.ShapeDtypeStruct((r, d), BF16),
                   jax.ShapeDtypeStruct((r, LANES), F32)),
        grid=(r // tm,),
        in_specs=[pl.BlockSpec((tm, d), lambda i: (i, 0)),
                  pl.BlockSpec((tm, d), lambda i: (i, 0)),
                  pl.BlockSpec(mod_l.shape, lambda i: (0, 0)),
                  pl.BlockSpec((None, 1, d), lambda i: (layer, 0, 0)),
                  pl.BlockSpec((None, d, d), lambda i: (layer, 0, 0)),
                  pl.BlockSpec((None, d, LANES), lambda i: (layer, 0, 0)),
                  pl.BlockSpec((None, 1, LANES), lambda i: (layer, 0, 0))],
        out_specs=(pl.BlockSpec((tm, d), lambda i: (i, 0)), pl.BlockSpec((tm, d), lambda i: (i, 0)),
                   pl.BlockSpec((tm, LANES), lambda i: (i, 0))),
        compiler_params=_cparams(("arbitrary",), 48),
        name="outproj",
    )(x2d, m, mod_l, norm_w, w_out_bf, w_router, b_router)


def _moe_kernel(ie_ref, ib_ref, in_ref, if_ref, x_hbm, wg_ref, wu_ref, wd_ref, bg_ref, bu_ref, bd_ref, y_hbm,
                xbuf, acc, sem_in, sem_out, *, nj):
    it, j = pl.program_id(0), pl.program_id(1)
    nblk = in_ref[it]
    nfill = if_ref[it]
    row0 = pl.multiple_of(ib_ref[it] * MOE_RB, MOE_RB)

    @pl.when((nfill > 0) & (j == 0))
    def _():
        acc[0:MOE_RB, :] = jnp.zeros((MOE_RB, acc.shape[1]), F32)

        def fill_copy(c):
            return pltpu.make_async_copy(acc.at[pl.ds(0, MOE_RB), :],
                                         y_hbm.at[pl.ds(row0 + c * MOE_RB, MOE_RB), :], sem_out)

        lax.fori_loop(0, nfill, lambda c, _: (fill_copy(c).start(), 0)[1], 0)
        lax.fori_loop(0, nfill, lambda c, _: (fill_copy(c).wait(), 0)[1], 0)

    def in_copy(c):
        return pltpu.make_async_copy(x_hbm.at[pl.ds(row0 + c * MOE_RB, MOE_RB), :],
                                     xbuf.at[pl.ds(c * MOE_RB, MOE_RB), :], sem_in)

    def out_copy(c):
        return pltpu.make_async_copy(acc.at[pl.ds(c * MOE_RB, MOE_RB), :],
                                     y_hbm.at[pl.ds(row0 + c * MOE_RB, MOE_RB), :], sem_out)

    @pl.when(nblk > 0)
    def _():
        @pl.when(j == 0)
        def _():
            lax.fori_loop(0, nblk, lambda c, _: (in_copy(c).start(), 0)[1], 0)
            lax.fori_loop(0, nblk, lambda c, _: (in_copy(c).wait(), 0)[1], 0)

        wg = wg_ref[...].astype(BF16)
        wu = wu_ref[...].astype(BF16)
        wd = wd_ref[...].astype(BF16)

        def block(c, _):
            rows = pl.ds(pl.multiple_of(c * MOE_RB, MOE_RB), MOE_RB)
            xb = xbuf[rows, :]
            gate = jnp.minimum(_dot(xb, wg) + bg_ref[...], SWIGLU_LIMIT)
            up = jnp.clip(_dot(xb, wu) + bu_ref[...], -SWIGLU_LIMIT, SWIGLU_LIMIT)
            a = gate * _sigmoid(SWIGLU_ALPHA * gate) * (up + 1.0)
            part = _dot(a.astype(BF16), wd)

            @pl.when(j == 0)
            def _():
                acc[rows, :] = part + bd_ref[...]

            @pl.when(j > 0)
            def _():
                acc[rows, :] += part

            return 0

        lax.fori_loop(0, nblk, block, 0)

        @pl.when(j == nj - 1)
        def _():
            lax.fori_loop(0, nblk, lambda c, _: (out_copy(c).start(), 0)[1], 0)
            lax.fori_loop(0, nblk, lambda c, _: (out_copy(c).wait(), 0)[1], 0)


def _moe_experts(item_e, item_blk, item_n, item_fill, x_sorted, w_gate, b_gate, w_up, b_up, w_down, b_down, layer):
    cap, d = x_sorted.shape
    f = w_gate.shape[-1]
    nj = f // MOE_TF
    n_items = item_e.shape[0]
    depth, ne = b_gate.shape[:2]

    def jsel(it, j, n):
        return jnp.where(n[it] > 0, j, nj - 1)

    w_in = lambda: pl.BlockSpec((None, None, d, MOE_TF), lambda it, j, e, b, n, fl: (layer, e[it], 0, jsel(it, j, n)))
    b_in = lambda: pl.BlockSpec((None, None, 1, MOE_TF), lambda it, j, e, b, n, fl: (layer, e[it], 0, jsel(it, j, n)))
    grid_spec = pltpu.PrefetchScalarGridSpec(
        num_scalar_prefetch=4,
        grid=(n_items, nj),
        in_specs=[pl.BlockSpec(memory_space=pl.ANY), w_in(), w_in(),
                  pl.BlockSpec((None, None, MOE_TF, d), lambda it, j, e, b, n, fl: (layer, e[it], jsel(it, j, n), 0)),
                  b_in(), b_in(),
                  pl.BlockSpec((None, None, 1, d), lambda it, j, e, b, n, fl: (layer, e[it], 0, 0))],
        out_specs=pl.BlockSpec(memory_space=pl.ANY),
        scratch_shapes=[pltpu.VMEM((MOE_KB * MOE_RB, d), BF16), pltpu.VMEM((MOE_KB * MOE_RB, d), F32),
                        pltpu.SemaphoreType.DMA, pltpu.SemaphoreType.DMA],
    )
    return pl.pallas_call(
        functools.partial(_moe_kernel, nj=nj),
        out_shape=jax.ShapeDtypeStruct((cap, d), F32),
        grid_spec=grid_spec,
        compiler_params=_cparams(("arbitrary", "arbitrary"), 56),
        name="moe_experts",
    )(item_e, item_blk, item_n, item_fill, x_sorted, w_gate, w_up, w_down,
      b_gate.reshape(depth, ne, 1, f), b_up.reshape(depth, ne, 1, f), b_down.reshape(depth, ne, 1, d))


def _moe_plan(logits, n_tok):
    top_val, top_idx = lax.top_k(logits, TOP_K)
    top_w = jax.nn.softmax(top_val, axis=-1)
    n_assign = n_tok * TOP_K
    exp_id = top_idx.reshape(-1)
    onehot = (exp_id[:, None] == jnp.arange(N_EXPERTS)[None, :]).astype(jnp.int32)
    csum = jnp.cumsum(onehot, axis=0)
    rank = jnp.sum((csum - onehot) * onehot, axis=1)
    counts = csum[-1]
    nblk = (counts + MOE_RB - 1) // MOE_RB
    blk_end = jnp.cumsum(nblk)
    blk_start = blk_end - nblk
    dest = blk_start[exp_id] * MOE_RB + rank
    total_blocks = (n_assign + N_EXPERTS * (MOE_RB - 1)) // MOE_RB
    cap = total_blocks * MOE_RB
    slot_tok = jnp.zeros((cap,), jnp.int32).at[dest].set(jnp.arange(n_assign, dtype=jnp.int32) // TOP_K)

    n_items = N_EXPERTS + total_blocks // MOE_KB
    items_e = (nblk + MOE_KB - 1) // MOE_KB
    item_end = jnp.cumsum(items_e)
    item_start = item_end - items_e
    idx = jnp.arange(n_items)
    e_of = jnp.minimum(jnp.searchsorted(item_end, idx, side="right"), N_EXPERTS - 1).astype(jnp.int32)
    valid = idx < item_end[-1]
    e_last = e_of[jnp.maximum(item_end[-1] - 1, 0)]
    kk = idx - item_start[e_of]
    item_e = jnp.where(valid, e_of, e_last).astype(jnp.int32)
    fill_blk = blk_end[-1] + (idx - item_end[-1]) * MOE_KB
    item_blk = jnp.where(valid, blk_start[e_of] + kk * MOE_KB, jnp.minimum(fill_blk, total_blocks)).astype(jnp.int32)
    item_n = jnp.where(valid, jnp.clip(nblk[e_of] - kk * MOE_KB, 0, MOE_KB), 0).astype(jnp.int32)
    item_fill = jnp.where(valid, 0, jnp.clip(total_blocks - fill_blk, 0, MOE_KB)).astype(jnp.int32)
    return top_w, dest.reshape(n_tok, TOP_K), slot_tok, item_e, item_blk, item_n, item_fill


def kernel(x, c, ctx, c_ctx, w_mod, b_mod, norm1_w, norm2_w, w_in, hg_lb_logits, hg_norm_w, na_q_norm_w,
           na_k_norm_w, na_rpb, ssd_conv_w, ssd_conv_b, ssd_dt_bias, ssd_a_log, ssd_d, ssd_norm_w, w_branch_hg,
           w_branch_na, w_branch_ssd, w_out, moe_w_router, moe_b_router, moe_w_gate, moe_b_gate, moe_w_up,
           moe_b_up, moe_w_down, moe_b_down):
    batch, seq, d = x.shape
    n_ctx = ctx.shape[1]
    depth = w_mod.shape[0]
    lt = n_ctx + seq
    r = batch * lt
    chunks = dict(chunks_per_seq=lt // MOD_CHUNK, ctx_chunks=n_ctx // MOD_CHUNK, batch=batch)
    assert n_ctx % MOD_CHUNK == 0 and seq % MOD_CHUNK == 0 and batch < SUBLANES

    lb_soft = jax.nn.softmax(hg_lb_logits.astype(F32), axis=0)
    lower_bounds = jnp.cumsum(lb_soft, axis=0) - lb_soft[0]
    w_proj_bf = w_in[:, :, :N_PROJ].astype(BF16)
    w_gate_bf = w_in[:, :, N_MIX_COLS:].astype(BF16)
    wb_hg, wb_na, wb_ssd = (w.astype(BF16) for w in (w_branch_hg, w_branch_na, w_branch_ssd))
    w_out_bf = w_out.astype(BF16)
    w_router = jnp.pad(moe_w_router, ((0, 0), (0, 0), (0, LANES - N_EXPERTS)))
    b_router = jnp.pad(moe_b_router, ((0, 0), (0, LANES - N_EXPERTS))).reshape(depth, 1, LANES)
    vec = lambda p: p.reshape(depth, 1, p.shape[-1])
    dt_pad = lambda p: jnp.pad(p.reshape(depth, 2 * SSD_HEADS), ((0, 0), (0, LANES - 2 * SSD_HEADS)))
    dt_bias_p, a_log_p = dt_pad(ssd_dt_bias), dt_pad(ssd_a_log)
    d_lanes = jnp.repeat(ssd_d, SSD_HD, axis=-1).reshape(depth, 1, SSD_WIDTH)
    cos_t, sin_t = _rope_tables(seq)
    conv_b = vec(ssd_conv_b)

    cvecs = jnp.zeros((SUBLANES, d), F32).at[:batch].set(c).at[batch].set(c_ctx)
    mod = _mod_table(cvecs, w_mod, b_mod)

    xa = jnp.concatenate([ctx, x], axis=1).reshape(r, d)
    for layer in range(depth):
        mod_l = mod[layer]
        y2d, h = _inproj(xa, mod_l, vec(norm1_w), w_proj_bf, layer, **chunks)
        y3 = y2d.reshape(batch, lt, N_PROJ)
        y_hg = _hgrn2(y3, lower_bounds[layer], vec(hg_norm_w), layer, n_ctx)
        bias_tab = _na_bias_table(na_rpb[layer], seq // GRID_W)
        y_na = _nattn(y3, bias_tab, vec(na_q_norm_w), vec(na_k_norm_w), layer, n_ctx)
        y_ssd = _ssd(y3, ssd_conv_w, conv_b, dt_bias_p[layer:layer + 1], a_log_p[layer:layer + 1], d_lanes,
                     cos_t, sin_t, layer, n_ctx)
        m = _mix(h, y_hg.reshape(r, -1), y_na.reshape(r, -1), y_ssd.reshape(r, -1), y2d, vec(ssd_norm_w),
                 wb_hg, wb_na, wb_ssd, w_gate_bf, layer)
        xa, h2, logits = _outproj(xa, m, mod_l, vec(norm2_w), w_out_bf, w_router, b_router, layer, **chunks)

        top_w, dest, slot_tok, item_e, item_blk, item_n, item_fill = _moe_plan(logits[:, :N_EXPERTS], r)
        x_sorted = jnp.take(h2, slot_tok, axis=0)
        y_sorted = _moe_experts(item_e, item_blk, item_n, item_fill, x_sorted, moe_w_gate, moe_b_gate, moe_w_up, moe_b_up,
                                moe_w_down, moe_b_down, layer)
        f = jnp.sum(jnp.take(y_sorted, dest, axis=0) * top_w[:, :, None], axis=1)
        g2 = jnp.concatenate([jnp.broadcast_to(mod_l[batch, 5 * d:], (batch, n_ctx, d)),
                              jnp.broadcast_to(mod_l[:batch, None, 5 * d:], (batch, seq, d))], axis=1)
        xa = xa + g2.reshape(r, d) * f
    return xa.reshape(batch, lt, d)[:, n_ctx:, :]
```

```python
import functools
import math

import numpy as np
import jax
import jax.numpy as jnp
from jax import lax
from jax.experimental import pallas as pl
from jax.experimental.pallas import tpu as pltpu

F32 = jnp.float32
BF16 = jnp.bfloat16

LANES = 128
SUBLANES = 8
VMEM_BYTES_V7X = 64 * 1024 * 1024

GRID_W = 64
EPS = 1e-6
NEG_BIG = -1e30
MIN_FORGET = 1e-6
HG_HEADS, HG_DK = 6, 128
HG_WIDTH = HG_HEADS * HG_DK
HG_CHUNK = 64
HG_BLK = 8
HG_EXP_CAP = 60.0
NA_HEADS, NA_HD = 4, 128
NA_WIDTH = NA_HEADS * NA_HD
NA_WIN_ROWS, NA_WIN_COLS = 8, 16
SSD_HEADS, SSD_HD = 12, 64
SSD_WIDTH = SSD_HEADS * SSD_HD
SSD_GROUPS, SSD_STATE, SSD_CONV = 4, 128, 4
SSD_CHUNK = 128
SSD_CONV_CH = SSD_WIDTH + 2 * SSD_GROUPS * SSD_STATE
ROPE_BASE = 10000.0
N_EXPERTS, TOP_K = 32, 4
SWIGLU_ALPHA, SWIGLU_LIMIT = 1.702, 7.0

C_HG = 0
C_NA = 5 * HG_WIDTH
C_Z = C_NA + 3 * NA_WIDTH
C_XBC = C_Z + SSD_WIDTH
C_DT = C_XBC + SSD_CONV_CH
N_MIX_COLS = C_DT + 2 * SSD_HEADS
N_PROJ = 8192

MOD_CHUNK = 256
MOE_RB = 256
MOE_KB = 5
MOE_TF = 512


def _cparams(sem, vmem_mb):
    return pltpu.CompilerParams(dimension_semantics=sem, vmem_limit_bytes=vmem_mb * 1024 * 1024)


def _dot(a, b):
    return jnp.dot(a, b, preferred_element_type=F32)


def _dot_nt(a, b):
    return lax.dot_general(a, b, (((1,), (1,)), ((), ())), preferred_element_type=F32)


def _dot_tn(a, b):
    return lax.dot_general(a, b, (((0,), (0,)), ((), ())), preferred_element_type=F32)


def _sigmoid(x):
    return 1.0 / (1.0 + jnp.exp(-x))


def _silu(x):
    return x * _sigmoid(x)


def _split3(x):
    a = x.astype(BF16)
    r = x - a.astype(F32)
    b = r.astype(BF16)
    c = (r - b.astype(F32)).astype(BF16)
    return a, b, c


def _tri_cumsum(tri, g):
    a, b, c = _split3(g)
    return _dot(tri, a) + _dot(tri, b) + _dot(tri, c)


def _scan_chunk(step, nchunks, nctx, reverse):
    if not reverse:
        return step
    return jnp.where(step < nctx, nctx - 1 - step, nchunks - 1 - (step - nctx))


def _tri_matrix(n, reverse):
    t = lax.broadcasted_iota(jnp.int32, (n, n), 0)
    s = lax.broadcasted_iota(jnp.int32, (n, n), 1)
    keep = (s >= t) if reverse else (s <= t)
    return keep, jnp.where(keep, 1.0, 0.0).astype(BF16)


def _mod_kernel(c_ref, w_ref, b_ref, o_ref):
    a = _silu(c_ref[...]).astype(BF16)
    o_ref[...] = _dot(a, w_ref[...].astype(BF16)) + b_ref[...]


def _mod_table(cvecs, w_mod, b_mod):
    depth, d, n = w_mod.shape
    tn = 1024
    return pl.pallas_call(
        _mod_kernel,
        out_shape=jax.ShapeDtypeStruct((depth, SUBLANES, n), F32),
        grid=(depth, n // tn),
        in_specs=[
            pl.BlockSpec((SUBLANES, d), lambda l, j: (0, 0)),
            pl.BlockSpec((None, d, tn), lambda l, j: (l, 0, j)),
            pl.BlockSpec((None, 1, tn), lambda l, j: (l, 0, j)),
        ],
        out_specs=pl.BlockSpec((None, SUBLANES, tn), lambda l, j: (l, 0, j)),
        compiler_params=_cparams(("arbitrary", "arbitrary"), 40),
        name="mod_table",
    )(cvecs, w_mod, b_mod.reshape(depth, 1, n))


def _mod_row_index(chunk, chunks_per_seq, ctx_chunks, batch):
    within = chunk % chunks_per_seq
    return jnp.where(within < ctx_chunks, batch, chunk // chunks_per_seq)


def _rms(x, w):
    return x * lax.rsqrt(jnp.mean(x * x, axis=-1, keepdims=True) + EPS) * w


def _inproj_kernel(x_ref, mod_ref, nw_ref, w_ref, y_ref, h_ref, *, tm, d, chunks_per_seq, ctx_chunks, batch):
    i, j = pl.program_id(0), pl.program_id(1)

    @pl.when(j == 0)
    def _():
        for s in range(tm // MOD_CHUNK):
            row = _mod_row_index(i * (tm // MOD_CHUNK) + s, chunks_per_seq, ctx_chunks, batch)
            shift = mod_ref[pl.ds(row, 1), 0:d]
            scale = mod_ref[pl.ds(row, 1), d:2 * d]
            xs = x_ref[s * MOD_CHUNK:(s + 1) * MOD_CHUNK, :]
            h = _rms(xs, nw_ref[...]) * (1.0 + scale) + shift
            h_ref[s * MOD_CHUNK:(s + 1) * MOD_CHUNK, :] = h.astype(BF16)

    y_ref[...] = _dot(h_ref[...], w_ref[...])


def _inproj(x2d, mod_l, norm_w, w_in_bf, layer, *, chunks_per_seq, ctx_chunks, batch):
    r, d = x2d.shape
    tm, tn = 1024, 1024
    assert r % tm == 0 and N_PROJ % tn == 0
    kern = functools.partial(_inproj_kernel, tm=tm, d=d, chunks_per_seq=chunks_per_seq,
                             ctx_chunks=ctx_chunks, batch=batch)
    return pl.pallas_call(
        kern,
        out_shape=(jax.ShapeDtypeStruct((r, N_PROJ), F32), jax.ShapeDtypeStruct((r, d), BF16)),
        grid=(r // tm, N_PROJ // tn),
        in_specs=[
            pl.BlockSpec((tm, d), lambda i, j: (i, 0)),
            pl.BlockSpec(mod_l.shape, lambda i, j: (0, 0)),
            pl.BlockSpec((None, 1, d), lambda i, j: (layer, 0, 0)),
            pl.BlockSpec((None, d, tn), lambda i, j: (layer, 0, j)),
        ],
        out_specs=(pl.BlockSpec((tm, tn), lambda i, j: (i, j)),
                   pl.BlockSpec((tm, d), lambda i, j: (i, 0))),
        compiler_params=_cparams(("arbitrary", "arbitrary"), 52),
        name="inproj",
    )(x2d, mod_l, norm_w, w_in_bf)


def _hg_chunk(q, k, v, cum, st, keep, reverse):
    c = q.shape[0]
    tot = cum[0:1] if reverse else cum[c - 1:c]
    qe = (q * jnp.exp(cum)).astype(BF16)
    o_inter = _dot_nt(qe, st.astype(BF16))
    kd = k * jnp.exp(tot - cum)
    st_new = st * jnp.exp(tot) + _dot_tn(v.astype(BF16), kd.astype(BF16))

    rows = []
    for j in range(c // HG_BLK):
        lo = j * HG_BLK
        mid = lo + HG_BLK // 2 - (0 if reverse else 1)
        r = cum[mid:mid + 1]
        qt = (q[lo:lo + HG_BLK] * jnp.exp(cum[lo:lo + HG_BLK] - r)).astype(BF16)
        kh = (k * jnp.exp(jnp.minimum(r - cum, HG_EXP_CAP))).astype(BF16)
        rows.append(_dot_nt(qt, kh))
    a = jnp.where(keep, jnp.concatenate(rows, axis=0), 0.0)
    return _dot(a.astype(BF16), v.astype(BF16)) + o_inter, st_new


def _hgrn2_kernel(q_ref, ff_ref, fb_ref, i_ref, g_ref, lb_ref, nw_ref, o_ref, accf_ref, accb_ref, *, ctx, lt):
    nchunks = lt // HG_CHUNK
    nctx = ctx // HG_CHUNK
    dirs = []
    for direction, (f_ref, acc_ref) in enumerate(((ff_ref, accf_ref), (fb_ref, accb_ref))):
        keep, tri = _tri_matrix(HG_CHUNK, direction == 1)
        dirs.append((direction == 1, f_ref, acc_ref, keep, tri, lb_ref[direction:direction + 1, :]))

    def body(step, states):
        new_states = []
        for (reverse, f_ref, acc_ref, keep, tri, lbd), st in zip(dirs, states):
            ci = _scan_chunk(step, nchunks, nctx, reverse)
            rows = pl.ds(pl.multiple_of(ci * HG_CHUNK, HG_CHUNK), HG_CHUNK)
            q = _silu(q_ref[rows, :]) * (HG_DK ** -0.5)
            f = lbd + (1.0 - lbd) * _sigmoid(f_ref[rows, :])
            logf = jnp.log(jnp.maximum(f, MIN_FORGET))
            k = 1.0 - f
            v = i_ref[rows, :]
            cum = _tri_cumsum(tri, logf)
            o, st_new = _hg_chunk(q, k, v, cum, st, keep, reverse)
            acc_ref[rows, :] = o
            new_states.append(st_new)
        return tuple(new_states)

    zero = jnp.zeros((HG_DK, HG_DK), F32)
    lax.fori_loop(0, nchunks, body, (zero, zero))

    y = accf_ref[...] + accb_ref[...]
    o_ref[...] = (_rms(y, nw_ref[...]) * _silu(g_ref[...])).astype(o_ref.dtype)


def _hgrn2(y3, lb, norm_w, layer, ctx):
    b, lt, _ = y3.shape
    blk = lambda off: pl.BlockSpec((None, lt, HG_DK), lambda bi, h, off=off: (bi, 0, off + h))
    return pl.pallas_call(
        functools.partial(_hgrn2_kernel, ctx=ctx, lt=lt),
        out_shape=jax.ShapeDtypeStruct((b, lt, HG_WIDTH), BF16),
        grid=(b, HG_HEADS),
        in_specs=[blk(0), blk(HG_HEADS), blk(2 * HG_HEADS), blk(3 * HG_HEADS), blk(4 * HG_HEADS),
                  pl.BlockSpec((2, HG_DK), lambda bi, h: (0, h)),
                  pl.BlockSpec((None, 1, HG_DK), lambda bi, h: (layer, 0, 0))],
        out_specs=pl.BlockSpec((None, lt, HG_DK), lambda bi, h: (bi, 0, h)),
        scratch_shapes=[pltpu.VMEM((lt, HG_DK), F32)] * 2,
        compiler_params=_cparams(("arbitrary", "arbitrary"), 40),
        name="hgrn2",
    )(y3, y3, y3, y3, y3, lb, norm_w)


def _na_kernel(q_ref, k_ref, v_ref, bias_ref, qw_ref, kw_ref, o_ref, qs_ref, ks_ref, vs_ref, *, ctx, seq):
    rows = seq // GRID_W
    wr = min(NA_WIN_ROWS, rows)
    scale = NA_HD ** -0.5
    qs_ref[...] = (_rms(q_ref[...], qw_ref[...]) * scale).astype(BF16)
    ks_ref[...] = _rms(k_ref[...], kw_ref[...]).astype(BF16)
    vs_ref[...] = v_ref[...].astype(BF16)
    kc = ks_ref[0:ctx, :]
    vc = vs_ref[0:ctx, :]

    s = _dot_nt(qs_ref[0:ctx, :], kc)
    p = jnp.exp(s - jnp.max(s, axis=-1, keepdims=True))
    o = _dot(p.astype(BF16), vc) / jnp.sum(p, axis=-1, keepdims=True)
    o_ref[0:ctx, :] = o.astype(o_ref.dtype)

    def body(r, carry):
        r0 = jnp.clip(r - wr // 2, 0, rows - wr)
        qrow = pl.ds(pl.multiple_of(ctx + r * GRID_W, GRID_W), GRID_W)
        krow = pl.ds(pl.multiple_of(ctx + r0 * GRID_W, GRID_W), wr * GRID_W)
        qr = qs_ref[qrow, :]
        s_lat = _dot_nt(qr, ks_ref[krow, :]) + bias_ref[r0 - r + NA_WIN_ROWS - 1]
        s_ctx = _dot_nt(qr, kc)
        m = jnp.maximum(jnp.max(s_lat, axis=-1, keepdims=True), jnp.max(s_ctx, axis=-1, keepdims=True))
        p_lat = jnp.exp(s_lat - m)
        p_ctx = jnp.exp(s_ctx - m)
        den = jnp.sum(p_lat, axis=-1, keepdims=True) + jnp.sum(p_ctx, axis=-1, keepdims=True)
        o = (_dot(p_lat.astype(BF16), vs_ref[krow, :]) + _dot(p_ctx.astype(BF16), vc)) / den
        o_ref[qrow, :] = o.astype(o_ref.dtype)
        return carry

    lax.fori_loop(0, rows, body, 0)


def _na_bias_table(rpb, rows):
    wr = min(NA_WIN_ROWS, rows)
    j = np.arange(GRID_W)
    c0 = np.clip(j - NA_WIN_COLS // 2, 0, GRID_W - NA_WIN_COLS)
    col_mask = (j[None, :] >= c0[:, None]) & (j[None, :] < c0[:, None] + NA_WIN_COLS)
    dc = np.clip(j[None, :] - j[:, None], -(NA_WIN_COLS - 1), NA_WIN_COLS - 1) + NA_WIN_COLS - 1
    onehot = jnp.asarray(dc[:, :, None] == np.arange(2 * NA_WIN_COLS - 1)[None, None, :], F32)
    base = jnp.einsum("hrd,qcd->hrqc", rpb.astype(F32), onehot, precision=lax.Precision.HIGHEST)
    base = jnp.where(col_mask[None, None], base, NEG_BIG)
    t = jnp.stack([base[:, i:i + wr] for i in range(NA_WIN_ROWS)], axis=1)
    return t.transpose(0, 1, 3, 2, 4).reshape(rpb.shape[0], NA_WIN_ROWS, GRID_W, wr * GRID_W)


def _nattn(y3, bias_tab, qw, kw, layer, ctx):
    b, lt, _ = y3.shape
    seq = lt - ctx
    base = C_NA // NA_HD
    blk = lambda off: pl.BlockSpec((None, lt, NA_HD), lambda bi, h, off=off: (bi, 0, base + off + h))
    return pl.pallas_call(
        functools.partial(_na_kernel, ctx=ctx, seq=seq),
        out_shape=jax.ShapeDtypeStruct((b, lt, NA_WIDTH), BF16),
        grid=(b, NA_HEADS),
        in_specs=[blk(0), blk(NA_HEADS), blk(2 * NA_HEADS),
                  pl.BlockSpec((None,) + bias_tab.shape[1:], lambda bi, h: (h, 0, 0, 0)),
                  pl.BlockSpec((None, 1, NA_HD), lambda bi, h: (layer, 0, 0)),
                  pl.BlockSpec((None, 1, NA_HD), lambda bi, h: (layer, 0, 0))],
        out_specs=pl.BlockSpec((None, lt, NA_HD), lambda bi, h: (bi, 0, h)),
        scratch_shapes=[pltpu.VMEM((lt, NA_HD), BF16)] * 3,
        compiler_params=_cparams(("arbitrary", "arbitrary"), 40),
        name="nattn",
    )(y3, y3, y3, bias_tab, qw, kw)


def _rope_tables(seq):
    half = SSD_STATE // 2
    pos = np.arange(seq)
    inv_freq = 1.0 / (ROPE_BASE ** (np.arange(0, half, 2, dtype=np.float64) / half))
    lane = np.arange(SSD_STATE)
    p = np.where(lane[None, :] < half, (pos // GRID_W)[:, None], (pos % GRID_W)[:, None])
    ang = p * inv_freq[lane % (half // 2)][None, :]
    sign = np.where((lane % half) < half // 2, -1.0, 1.0)
    return jnp.asarray(np.cos(ang), F32), jnp.asarray(np.sin(ang) * sign[None, :], F32)


def _conv_silu(x, w, bias, ctx, lt):
    t = lax.broadcasted_iota(jnp.int32, (lt, 1), 0)
    pos = jnp.where(t < ctx, t, t - ctx)
    length = jnp.where(t < ctx, ctx, lt - ctx)
    left = (SSD_CONV - 1) // 2
    acc = bias + jnp.zeros_like(x)
    for j in range(SSD_CONV):
        off = j - left
        xs = x if off == 0 else pltpu.roll(x, (-off) % lt, axis=0)
        ok = (pos + off >= 0) & (pos + off < length)
        acc = acc + jnp.where(ok, xs, 0.0) * w[j:j + 1, :]
    return _silu(acc)


def _rope(t, cos, sin_signed):
    lane = lax.broadcasted_iota(jnp.int32, t.shape, 1)
    quarter = SSD_STATE // 4
    swapped = jnp.where((lane & quarter) == 0, pltpu.roll(t, SSD_STATE - quarter, axis=1),
                        pltpu.roll(t, quarter, axis=1))
    return t * cos + swapped * sin_signed


def _ssd_kernel(x_ref, b0_ref, b1_ref, c0_ref, c1_ref, dt_ref, wx_ref, wb0_ref, wb1_ref, wc0_ref, wc1_ref,
                bx_ref, bb0_ref, bb1_ref, bc0_ref, bc1_ref, dtb_ref, alog_ref, dskip_ref, cos_ref, sin_ref,
                o_ref, xs_ref, bs0_ref, bs1_ref, cs0_ref, cs1_ref, dts_ref, yb_ref, *, ctx, lt):
    nchunks = lt // SSD_CHUNK
    nctx = ctx // SSD_CHUNK
    pair = pl.program_id(1)

    xs_ref[...] = _conv_silu(x_ref[...], wx_ref[...], bx_ref[...], ctx, lt)
    for src, w, bb, dst in ((b0_ref, wb0_ref, bb0_ref, bs0_ref), (b1_ref, wb1_ref, bb1_ref, bs1_ref),
                            (c0_ref, wc0_ref, bc0_ref, cs0_ref), (c1_ref, wc1_ref, bc1_ref, cs1_ref)):
        t = _conv_silu(src[...], w[...], bb[...], ctx, lt)
        dst[0:ctx, :] = t[0:ctx].astype(BF16)
        dst[ctx:lt, :] = _rope(t[ctx:lt], cos_ref[...], sin_ref[...]).astype(BF16)
    raw = dt_ref[...] + dtb_ref[...]
    dts_ref[...] = jnp.maximum(raw, 0.0) + jnp.log(1.0 + jnp.exp(-jnp.abs(raw)))

    lane = lax.broadcasted_iota(jnp.int32, (SSD_CHUNK, LANES), 1)
    a_neg = -jnp.exp(alog_ref[...])
    masks = [_tri_matrix(SSD_CHUNK, reverse) for reverse in (False, True)]

    def body(step, states):
        new_states = []
        for direction in range(2):
            reverse = direction == 1
            keep, tri = masks[direction]
            ci = _scan_chunk(step, nchunks, nctx, reverse)
            rows = pl.ds(pl.multiple_of(ci * SSD_CHUNK, SSD_CHUNK), SSD_CHUNK)
            x = xs_ref[rows, :]
            dt_all = dts_ref[rows, :]
            da_all = dt_all * a_neg
            y = jnp.zeros((SSD_CHUNK, LANES), F32)
            for hh, (bs_ref, cs_ref) in enumerate(((bs0_ref, cs0_ref), (bs1_ref, cs1_ref))):
                chan = direction * SSD_HEADS + 2 * pair + hh
                sel = lane == chan
                dt_c = jnp.sum(jnp.where(sel, dt_all, 0.0), axis=-1, keepdims=True)
                da_c = jnp.sum(jnp.where(sel, da_all, 0.0), axis=-1, keepdims=True)
                dt_cb = jnp.broadcast_to(dt_c, (SSD_CHUNK, LANES))
                cum_cb = _tri_cumsum(tri, jnp.broadcast_to(da_c, (SSD_CHUNK, LANES)))
                cum_rb = cum_cb.T
                dt_rb = dt_cb.T
                tot = cum_cb[0:1] if reverse else cum_cb[SSD_CHUNK - 1:SSD_CHUNK]
                seg = jnp.where(keep, jnp.exp(jnp.where(keep, cum_cb - cum_rb, 0.0)), 0.0)
                bm = bs_ref[rows, :]
                cm = cs_ref[rows, :]
                m = _dot_nt(cm, bm) * seg * dt_rb
                half_mask = (lane // SSD_HD) == hh
                xm = jnp.where(half_mask, x, 0.0)
                st = states[2 * direction + hh]
                yh = _dot(m.astype(BF16), xm.astype(BF16)) + _dot(cm, st.astype(BF16)) * jnp.exp(cum_cb)
                w_end = dt_cb * jnp.exp(tot - cum_cb)
                st_new = st * jnp.exp(tot) + _dot_tn(bm, (xm * w_end).astype(BF16))
                y = y + yh
                new_states.append(st_new)
            if direction == 0:
                o_ref[rows, :] = y + dskip_ref[...] * x
            else:
                yb_ref[rows, :] = y
        return tuple(new_states)

    zero = jnp.zeros((SSD_STATE, LANES), F32)
    lax.fori_loop(0, nchunks, body, (zero,) * 4)
    o_ref[...] += yb_ref[...]


def _ssd(y3, conv_w, conv_b, dt_bias_l, a_log_l, d_lanes, cos_t, sin_t, layer, ctx):
    b, lt, _ = y3.shape
    npair = SSD_HEADS // 2
    hg = SSD_HEADS // SSD_GROUPS
    xb = C_XBC // LANES
    bb = xb + SSD_WIDTH // LANES
    cb = bb + SSD_GROUPS
    g0 = lambda p: (2 * p) // hg
    g1 = lambda p: (2 * p + 1) // hg
    yblk = lambda f: pl.BlockSpec((None, lt, LANES), lambda bi, p, f=f: (bi, 0, f(p)))
    wblk = lambda f: pl.BlockSpec((None, SSD_CONV, LANES), lambda bi, p, f=f: (layer, 0, f(p)))
    bblk = lambda f: pl.BlockSpec((None, 1, LANES), lambda bi, p, f=f: (layer, 0, f(p)))
    cols = [lambda p: p, lambda p: npair + g0(p), lambda p: npair + g1(p),
            lambda p: npair + SSD_GROUPS + g0(p), lambda p: npair + SSD_GROUPS + g1(p)]
    const2 = lambda shape: pl.BlockSpec(shape, lambda bi, p: (0, 0))
    return pl.pallas_call(
        functools.partial(_ssd_kernel, ctx=ctx, lt=lt),
        out_shape=jax.ShapeDtypeStruct((b, lt, SSD_WIDTH), F32),
        grid=(b, npair),
        in_specs=[yblk(lambda p: xb + p), yblk(lambda p: bb + g0(p)), yblk(lambda p: bb + g1(p)),
                  yblk(lambda p: cb + g0(p)), yblk(lambda p: cb + g1(p)), yblk(lambda p: C_DT // LANES)]
                 + [wblk(f) for f in cols] + [bblk(f) for f in cols]
                 + [const2((1, LANES)), const2((1, LANES)),
                    pl.BlockSpec((None, 1, LANES), lambda bi, p: (layer, 0, p)),
                    const2(cos_t.shape), const2(sin_t.shape)],
        out_specs=pl.BlockSpec((None, lt, LANES), lambda bi, p: (bi, 0, p)),
        scratch_shapes=[pltpu.VMEM((lt, LANES), F32)] + [pltpu.VMEM((lt, LANES), BF16)] * 4
                       + [pltpu.VMEM((lt, LANES), F32)] * 2,
        compiler_params=_cparams(("arbitrary", "arbitrary"), 48),
        name="ssd",
    )(y3, y3, y3, y3, y3, y3, *([conv_w] * 5), *([conv_b] * 5), dt_bias_l, a_log_l, d_lanes, cos_t, sin_t)


def _mix_kernel(h_ref, yh_ref, yn_ref, ys_ref, z_ref, snw_ref, wh_ref, wn_ref, ws_ref, g0_ref, g1_ref, g2_ref,
                m_ref, ysn_ref):
    @pl.when(pl.program_id(1) == 0)
    def _():
        y = ys_ref[...] * _silu(z_ref[...])
        gw = SSD_WIDTH // SSD_GROUPS
        lane = lax.broadcasted_iota(jnp.int32, y.shape, 1)
        inv = jnp.zeros_like(y)
        for g in range(SSD_GROUPS):
            in_g = (lane >= g * gw) & (lane < (g + 1) * gw)
            ms = jnp.sum(jnp.where(in_g, y * y, 0.0), axis=-1, keepdims=True) * (1.0 / gw)
            inv = jnp.where(in_g, lax.rsqrt(ms + EPS), inv)
        ysn_ref[...] = (y * inv * snw_ref[...]).astype(BF16)

    h = h_ref[...]
    m = _sigmoid(_dot(h, g0_ref[...])) * _dot(yh_ref[...], wh_ref[...])
    m = m + _sigmoid(_dot(h, g1_ref[...])) * _dot(yn_ref[...], wn_ref[...])
    m = m + _sigmoid(_dot(h, g2_ref[...])) * _dot(ysn_ref[...], ws_ref[...])
    m_ref[...] = m.astype(m_ref.dtype)


def _mix(h, y_hg, y_na, y_ssd, y2d, ssd_nw, wb_hg, wb_na, wb_ssd, w_gate_bf, layer):
    r, d = h.shape
    tm, tn = 512, 512
    nt = d // tn
    row = lambda w: pl.BlockSpec((tm, w), lambda i, j: (i, 0))
    wspec = lambda k: pl.BlockSpec((None, k, tn), lambda i, j: (layer, 0, j))
    gspec = lambda br: pl.BlockSpec((None, d, tn), lambda i, j, br=br: (layer, 0, br * nt + j))
    return pl.pallas_call(
        _mix_kernel,
        out_shape=jax.ShapeDtypeStruct((r, d), BF16),
        grid=(r // tm, nt),
        in_specs=[row(d), row(HG_WIDTH), row(NA_WIDTH), row(SSD_WIDTH),
                  pl.BlockSpec((tm, SSD_WIDTH), lambda i, j: (i, C_Z // SSD_WIDTH)),
                  pl.BlockSpec((None, 1, SSD_WIDTH), lambda i, j: (layer, 0, 0)),
                  wspec(HG_WIDTH), wspec(NA_WIDTH), wspec(SSD_WIDTH), gspec(0), gspec(1), gspec(2)],
        out_specs=pl.BlockSpec((tm, tn), lambda i, j: (i, j)),
        scratch_shapes=[pltpu.VMEM((tm, SSD_WIDTH), BF16)],
        compiler_params=_cparams(("arbitrary", "arbitrary"), 48),
        name="mix",
    )(h, y_hg, y_na, y_ssd, y2d, ssd_nw, wb_hg, wb_na, wb_ssd, w_gate_bf, w_gate_bf, w_gate_bf)


def _outproj_kernel(x_ref, m_ref, mod_ref, nw_ref, wo_ref, wr_ref, br_ref, xo_ref, h2_ref, lg_ref,
                    *, d, chunks_per_seq, ctx_chunks, batch):
    row = _mod_row_index(pl.program_id(0), chunks_per_seq, ctx_chunks, batch)
    g1 = mod_ref[pl.ds(row, 1), 2 * d:3 * d]
    sh2 = mod_ref[pl.ds(row, 1), 3 * d:4 * d]
    sc2 = mod_ref[pl.ds(row, 1), 4 * d:5 * d]
    x = x_ref[...] + g1 * _dot(m_ref[...], wo_ref[...])
    xo_ref[...] = x
    h2 = _rms(x, nw_ref[...]) * (1.0 + sc2) + sh2
    h2_ref[...] = h2
    a, b, _ = _split3(h2)
    wa, wb, _ = _split3(wr_ref[...])
    lg_ref[...] = _dot(a, wa) + _dot(a, wb) + _dot(b, wa) + br_ref[...]


def _outproj(x2d, m, mod_l, norm_w, w_out_bf, w_router, b_router, layer, *, chunks_per_seq, ctx_chunks, batch):
    r, d = x2d.shape
    tm = MOD_CHUNK
    kern = functools.partial(_outproj_kernel, d=d, chunks_per_seq=chunks_per_seq, ctx_chunks=ctx_chunks,
                             batch=batch)
    return pl.pallas_call(
        kern,
        out_shape=(jax.ShapeDtypeStruct((r, d), F32), jax.ShapeDtypeStruct((r, d), F32),
                   jax.ShapeDtypeStruct((r, LANES), F32)),
        grid=(r // tm,),
        in_specs=[pl.BlockSpec((tm, d), lambda i: (i, 0)),
                  pl.BlockSpec((tm, d), lambda i: (i, 0)),
                  pl.BlockSpec(mod_l.shape, lambda i: (0, 0)),
                  pl.BlockSpec((None, 1, d), lambda i: (layer, 0, 0)),
                  pl.BlockSpec((None, d, d), lambda i: (layer, 0, 0)),
                  pl.BlockSpec((None, d, LANES), lambda i: (layer, 0, 0)),
                  pl.BlockSpec((None, 1, LANES), lambda i: (layer, 0, 0))],
        out_specs=(pl.BlockSpec((tm, d), lambda i: (i, 0)), pl.BlockSpec((tm, d), lambda i: (i, 0)),
                   pl.BlockSpec((tm, LANES), lambda i: (i, 0))),
        compiler_params=_cparams(("arbitrary",), 48),
        name="outproj",
    )(x2d, m, mod_l, norm_w, w_out_bf, w_router, b_router)


def _moe_kernel(ie_ref, ib_ref, in_ref, if_ref, x_hbm, wg_ref, wu_ref, wd_ref, bg_ref, bu_ref, bd_ref, y_hbm,
                xbuf, acc, sem_in, sem_out, *, nj, n_items):
    it, j = pl.program_id(0), pl.program_id(1)
    nblk = in_ref[it]
    nfill = if_ref[it]
    row0 = pl.multiple_of(ib_ref[it] * MOE_RB, MOE_RB)

    def in_copy(c):
        return pltpu.make_async_copy(x_hbm.at[pl.ds(row0 + c * MOE_RB, MOE_RB), :],
                                     xbuf.at[pl.ds(c * MOE_RB, MOE_RB), :], sem_in)

    def out_copy(c, src_block):
        return pltpu.make_async_copy(acc.at[pl.ds(src_block * MOE_RB, MOE_RB), :],
                                     y_hbm.at[pl.ds(row0 + c * MOE_RB, MOE_RB), :], sem_out)

    def wait_out(n):
        one = pltpu.make_async_copy(acc.at[pl.ds(0, MOE_RB), :], y_hbm.at[pl.ds(0, MOE_RB), :], sem_out)
        lax.fori_loop(0, n, lambda c, _: (one.wait(), 0)[1], 0)

    n_prev = jnp.where(it > 0, in_ref[jnp.maximum(it - 1, 0)], 0)

    @pl.when((j == 0) & (nblk == 0))
    def _():
        wait_out(n_prev)

    @pl.when((nfill > 0) & (j == 0))
    def _():
        acc[0:MOE_RB, :] = jnp.zeros((MOE_RB, acc.shape[1]), F32)
        lax.fori_loop(0, nfill, lambda c, _: (out_copy(c, 0).start(), 0)[1], 0)
        wait_out(nfill)

    @pl.when(nblk > 0)
    def _():
        @pl.when(j == 0)
        def _():
            lax.fori_loop(0, nblk, lambda c, _: (in_copy(c).start(), 0)[1], 0)

        wg = wg_ref[...].astype(BF16)
        wu = wu_ref[...].astype(BF16)
        wd = wd_ref[...].astype(BF16)

        @pl.when(j == 0)
        def _():
            lax.fori_loop(0, nblk, lambda c, _: (in_copy(c).wait(), 0)[1], 0)

        def block(c, _):
            rows = pl.ds(pl.multiple_of(c * MOE_RB, MOE_RB), MOE_RB)
            xb = xbuf[rows, :].astype(BF16)
            gate = jnp.minimum(_dot(xb, wg) + bg_ref[...], SWIGLU_LIMIT)
            up = jnp.clip(_dot(xb, wu) + bu_ref[...], -SWIGLU_LIMIT, SWIGLU_LIMIT)
            a = gate * _sigmoid(SWIGLU_ALPHA * gate) * (up + 1.0)
            part = _dot(a.astype(BF16), wd)

            @pl.when((j == 0) & (c == 0))
            def _():
                wait_out(n_prev)

            @pl.when(j == 0)
            def _():
                acc[rows, :] = part + bd_ref[...]

            @pl.when(j > 0)
            def _():
                acc[rows, :] += part

            return 0

        lax.fori_loop(0, nblk, block, 0)

        @pl.when(j == nj - 1)
        def _():
            lax.fori_loop(0, nblk, lambda c, _: (out_copy(c, c).start(), 0)[1], 0)

            @pl.when(it == n_items - 1)
            def _():
                wait_out(nblk)


def _moe_experts(item_e, item_blk, item_n, item_fill, x_sorted, w_gate, b_gate, w_up, b_up, w_down, b_down, layer):
    cap, d = x_sorted.shape
    f = w_gate.shape[-1]
    nj = f // MOE_TF
    n_items = item_e.shape[0]
    depth, ne = b_gate.shape[:2]

    def jsel(it, j, n):
        return jnp.where(n[it] > 0, j, nj - 1)

    w_in = lambda: pl.BlockSpec((None, None, d, MOE_TF), lambda it, j, e, b, n, fl: (layer, e[it], 0, jsel(it, j, n)))
    b_in = lambda: pl.BlockSpec((None, None, 1, MOE_TF), lambda it, j, e, b, n, fl: (layer, e[it], 0, jsel(it, j, n)))
    grid_spec = pltpu.PrefetchScalarGridSpec(
        num_scalar_prefetch=4,
        grid=(n_items, nj),
        in_specs=[pl.BlockSpec(memory_space=pl.ANY), w_in(), w_in(),
                  pl.BlockSpec((None, None, MOE_TF, d), lambda it, j, e, b, n, fl: (layer, e[it], jsel(it, j, n), 0)),
                  b_in(), b_in(),
                  pl.BlockSpec((None, None, 1, d), lambda it, j, e, b, n, fl: (layer, e[it], 0, 0))],
        out_specs=pl.BlockSpec(memory_space=pl.ANY),
        scratch_shapes=[pltpu.VMEM((MOE_KB * MOE_RB, d), F32), pltpu.VMEM((MOE_KB * MOE_RB, d), F32),
                        pltpu.SemaphoreType.DMA, pltpu.SemaphoreType.DMA],
    )
    return pl.pallas_call(
        functools.partial(_moe_kernel, nj=nj, n_items=n_items),
        out_shape=jax.ShapeDtypeStruct((cap, d), F32),
        grid_spec=grid_spec,
        compiler_params=_cparams(("arbitrary", "arbitrary"), 60),
        name="moe_experts",
    )(item_e, item_blk, item_n, item_fill, x_sorted, w_gate, w_up, w_down,
      b_gate.reshape(depth, ne, 1, f), b_up.reshape(depth, ne, 1, f), b_down.reshape(depth, ne, 1, d))


def _combine_kernel(x_ref, y_ref, w_ref, mod_ref, o_ref, *, d, chunks_per_seq, ctx_chunks, batch):
    row = _mod_row_index(pl.program_id(0), chunks_per_seq, ctx_chunks, batch)
    g2 = mod_ref[pl.ds(row, 1), 5 * d:6 * d]
    w = w_ref[...]
    f = y_ref[0] * w[:, 0:1]
    for k in range(1, TOP_K):
        f = f + y_ref[k] * w[:, k:k + 1]
    o_ref[...] = x_ref[...] + g2 * f


def _combine(x2d, yk, top_w, mod_l, *, chunks_per_seq, ctx_chunks, batch):
    r, d = x2d.shape
    tm = MOD_CHUNK
    kern = functools.partial(_combine_kernel, d=d, chunks_per_seq=chunks_per_seq, ctx_chunks=ctx_chunks,
                             batch=batch)
    return pl.pallas_call(
        kern,
        out_shape=jax.ShapeDtypeStruct((r, d), F32),
        grid=(r // tm,),
        in_specs=[pl.BlockSpec((tm, d), lambda i: (i, 0)),
                  pl.BlockSpec((TOP_K, tm, d), lambda i: (0, i, 0)),
                  pl.BlockSpec((tm, TOP_K), lambda i: (i, 0)),
                  pl.BlockSpec(mod_l.shape, lambda i: (0, 0))],
        out_specs=pl.BlockSpec((tm, d), lambda i: (i, 0)),
        compiler_params=_cparams(("arbitrary",), 40),
        name="combine",
    )(x2d, yk, top_w, mod_l)


def _moe_plan(logits, n_tok):
    top_val, top_idx = lax.top_k(logits, TOP_K)
    top_w = jax.nn.softmax(top_val, axis=-1)
    n_assign = n_tok * TOP_K
    exp_id = top_idx.reshape(-1)
    onehot = (exp_id[:, None] == jnp.arange(N_EXPERTS)[None, :]).astype(jnp.int32)
    csum = jnp.cumsum(onehot, axis=0)
    rank = jnp.sum((csum - onehot) * onehot, axis=1)
    counts = csum[-1]
    nblk = (counts + MOE_RB - 1) // MOE_RB
    blk_end = jnp.cumsum(nblk)
    blk_start = blk_end - nblk
    dest = blk_start[exp_id] * MOE_RB + rank
    total_blocks = (n_assign + N_EXPERTS * (MOE_RB - 1)) // MOE_RB
    cap = total_blocks * MOE_RB
    slot_tok = jnp.zeros((cap,), jnp.int32).at[dest].set(jnp.arange(n_assign, dtype=jnp.int32) // TOP_K)

    n_items = N_EXPERTS + total_blocks // MOE_KB
    items_e = (nblk + MOE_KB - 1) // MOE_KB
    item_end = jnp.cumsum(items_e)
    item_start = item_end - items_e
    idx = jnp.arange(n_items)
    e_of = jnp.minimum(jnp.searchsorted(item_end, idx, side="right"), N_EXPERTS - 1).astype(jnp.int32)
    valid = idx < item_end[-1]
    e_last = e_of[jnp.maximum(item_end[-1] - 1, 0)]
    kk = idx - item_start[e_of]
    item_e = jnp.where(valid, e_of, e_last).astype(jnp.int32)
    fill_blk = blk_end[-1] + (idx - item_end[-1]) * MOE_KB
    item_blk = jnp.where(valid, blk_start[e_of] + kk * MOE_KB, jnp.minimum(fill_blk, total_blocks)).astype(jnp.int32)
    item_n = jnp.where(valid, jnp.clip(nblk[e_of] - kk * MOE_KB, 0, MOE_KB), 0).astype(jnp.int32)
    item_fill = jnp.where(valid, 0, jnp.clip(total_blocks - fill_blk, 0, MOE_KB)).astype(jnp.int32)
    return top_w, dest.reshape(n_tok, TOP_K), slot_tok, item_e, item_blk, item_n, item_fill


def kernel(x, c, ctx, c_ctx, w_mod, b_mod, norm1_w, norm2_w, w_in, hg_lb_logits, hg_norm_w, na_q_norm_w,
           na_k_norm_w, na_rpb, ssd_conv_w, ssd_conv_b, ssd_dt_bias, ssd_a_log, ssd_d, ssd_norm_w, w_branch_hg,
           w_branch_na, w_branch_ssd, w_out, moe_w_router, moe_b_router, moe_w_gate, moe_b_gate, moe_w_up,
           moe_b_up, moe_w_down, moe_b_down):
    batch, seq, d = x.shape
    n_ctx = ctx.shape[1]
    depth = w_mod.shape[0]
    lt = n_ctx + seq
    r = batch * lt
    chunks = dict(chunks_per_seq=lt // MOD_CHUNK, ctx_chunks=n_ctx // MOD_CHUNK, batch=batch)
    assert n_ctx % MOD_CHUNK == 0 and seq % MOD_CHUNK == 0 and batch < SUBLANES

    lb_soft = jax.nn.softmax(hg_lb_logits.astype(F32), axis=0)
    lower_bounds = jnp.cumsum(lb_soft, axis=0) - lb_soft[0]
    w_proj_bf = w_in[:, :, :N_PROJ].astype(BF16)
    w_gate_bf = w_in[:, :, N_MIX_COLS:].astype(BF16)
    wb_hg, wb_na, wb_ssd = (w.astype(BF16) for w in (w_branch_hg, w_branch_na, w_branch_ssd))
    w_out_bf = w_out.astype(BF16)
    w_router = jnp.pad(moe_w_router, ((0, 0), (0, 0), (0, LANES - N_EXPERTS)))
    b_router = jnp.pad(moe_b_router, ((0, 0), (0, LANES - N_EXPERTS))).reshape(depth, 1, LANES)
    vec = lambda p: p.reshape(depth, 1, p.shape[-1])
    dt_pad = lambda p: jnp.pad(p.reshape(depth, 2 * SSD_HEADS), ((0, 0), (0, LANES - 2 * SSD_HEADS)))
    dt_bias_p, a_log_p = dt_pad(ssd_dt_bias), dt_pad(ssd_a_log)
    d_lanes = jnp.repeat(ssd_d, SSD_HD, axis=-1).reshape(depth, 1, SSD_WIDTH)
    cos_t, sin_t = _rope_tables(seq)
    conv_b = vec(ssd_conv_b)

    cvecs = jnp.zeros((SUBLANES, d), F32).at[:batch].set(c).at[batch].set(c_ctx)
    mod = _mod_table(cvecs, w_mod, b_mod)

    xa = jnp.concatenate([ctx, x], axis=1).reshape(r, d)
    for layer in range(depth):
        mod_l = mod[layer]
        y2d, h = _inproj(xa, mod_l, vec(norm1_w), w_proj_bf, layer, **chunks)
        y3 = y2d.reshape(batch, lt, N_PROJ)
        y_hg = _hgrn2(y3, lower_bounds[layer], vec(hg_norm_w), layer, n_ctx)
        bias_tab = _na_bias_table(na_rpb[layer], seq // GRID_W)
        y_na = _nattn(y3, bias_tab, vec(na_q_norm_w), vec(na_k_norm_w), layer, n_ctx)
        y_ssd = _ssd(y3, ssd_conv_w, conv_b, dt_bias_p[layer:layer + 1], a_log_p[layer:layer + 1], d_lanes,
                     cos_t, sin_t, layer, n_ctx)
        m = _mix(h, y_hg.reshape(r, -1), y_na.reshape(r, -1), y_ssd.reshape(r, -1), y2d, vec(ssd_norm_w),
                 wb_hg, wb_na, wb_ssd, w_gate_bf, layer)
        xa, h2, logits = _outproj(xa, m, mod_l, vec(norm2_w), w_out_bf, w_router, b_router, layer, **chunks)

        top_w, dest, slot_tok, item_e, item_blk, item_n, item_fill = _moe_plan(logits[:, :N_EXPERTS], r)
        x_sorted = jnp.take(h2, slot_tok, axis=0, mode="clip")
        y_sorted = _moe_experts(item_e, item_blk, item_n, item_fill, x_sorted, moe_w_gate, moe_b_gate, moe_w_up, moe_b_up,
                                moe_w_down, moe_b_down, layer)
        yk = jnp.take(y_sorted, dest.T, axis=0, mode="clip")
        xa = _combine(xa, yk, top_w, mod_l, **chunks)
    return xa.reshape(batch, lt, d)[:, n_ctx:, :]
```

```python
import functools
import math

import numpy as np
import jax
import jax.numpy as jnp
from jax import lax
from jax.experimental import pallas as pl
from jax.experimental.pallas import tpu as pltpu

F32 = jnp.float32
BF16 = jnp.bfloat16

LANES = 128
SUBLANES = 8
VMEM_BYTES_V7X = 64 * 1024 * 1024

GRID_W = 64
EPS = 1e-6
NEG_BIG = -1e30
MIN_FORGET = 1e-6
HG_HEADS, HG_DK = 6, 128
HG_WIDTH = HG_HEADS * HG_DK
HG_CHUNK = 64
HG_BLK = 8
HG_EXP_CAP = 60.0
NA_HEADS, NA_HD = 4, 128
NA_WIDTH = NA_HEADS * NA_HD
NA_WIN_ROWS, NA_WIN_COLS = 8, 16
SSD_HEADS, SSD_HD = 12, 64
SSD_WIDTH = SSD_HEADS * SSD_HD
SSD_GROUPS, SSD_STATE, SSD_CONV = 4, 128, 4
SSD_CHUNK = 128
SSD_CONV_CH = SSD_WIDTH + 2 * SSD_GROUPS * SSD_STATE
ROPE_BASE = 10000.0
N_EXPERTS, TOP_K = 32, 4
SWIGLU_ALPHA, SWIGLU_LIMIT = 1.702, 7.0

C_HG = 0
C_NA = 5 * HG_WIDTH
C_Z = C_NA + 3 * NA_WIDTH
C_XBC = C_Z + SSD_WIDTH
C_DT = C_XBC + SSD_CONV_CH
N_MIX_COLS = C_DT + 2 * SSD_HEADS
N_PROJ = 8192

MOD_CHUNK = 256
MOE_RB = 256
MOE_KB = 5
MOE_TF = 512


def _cparams(sem, vmem_mb):
    return pltpu.CompilerParams(dimension_semantics=sem, vmem_limit_bytes=vmem_mb * 1024 * 1024)


def _dot(a, b):
    return jnp.dot(a, b, preferred_element_type=F32)


def _dot_nt(a, b):
    return lax.dot_general(a, b, (((1,), (1,)), ((), ())), preferred_element_type=F32)


def _dot_tn(a, b):
    return lax.dot_general(a, b, (((0,), (0,)), ((), ())), preferred_element_type=F32)


def _sigmoid(x):
    return 1.0 / (1.0 + jnp.exp(-x))


def _silu(x):
    return x * _sigmoid(x)


def _split3(x):
    a = x.astype(BF16)
    r = x - a.astype(F32)
    b = r.astype(BF16)
    c = (r - b.astype(F32)).astype(BF16)
    return a, b, c


def _tri_cumsum(tri, g):
    a, b, c = _split3(g)
    return _dot(tri, a) + _dot(tri, b) + _dot(tri, c)


def _scan_chunk(step, nchunks, nctx, reverse):
    if not reverse:
        return step
    return jnp.where(step < nctx, nctx - 1 - step, nchunks - 1 - (step - nctx))


def _tri_matrix(n, reverse):
    t = lax.broadcasted_iota(jnp.int32, (n, n), 0)
    s = lax.broadcasted_iota(jnp.int32, (n, n), 1)
    keep = (s >= t) if reverse else (s <= t)
    return keep, jnp.where(keep, 1.0, 0.0).astype(BF16)


def _mod_kernel(c_ref, w_ref, b_ref, o_ref):
    a = _silu(c_ref[...]).astype(BF16)
    o_ref[...] = _dot(a, w_ref[...].astype(BF16)) + b_ref[...]


def _mod_table(cvecs, w_mod, b_mod):
    depth, d, n = w_mod.shape
    tn = 1024
    return pl.pallas_call(
        _mod_kernel,
        out_shape=jax.ShapeDtypeStruct((depth, SUBLANES, n), F32),
        grid=(depth, n // tn),
        in_specs=[
            pl.BlockSpec((SUBLANES, d), lambda l, j: (0, 0)),
            pl.BlockSpec((None, d, tn), lambda l, j: (l, 0, j)),
            pl.BlockSpec((None, 1, tn), lambda l, j: (l, 0, j)),
        ],
        out_specs=pl.BlockSpec((None, SUBLANES, tn), lambda l, j: (l, 0, j)),
        compiler_params=_cparams(("arbitrary", "arbitrary"), 40),
        name="mod_table",
    )(cvecs, w_mod, b_mod.reshape(depth, 1, n))


def _mod_row_index(chunk, chunks_per_seq, ctx_chunks, batch):
    within = chunk % chunks_per_seq
    return jnp.where(within < ctx_chunks, batch, chunk // chunks_per_seq)


def _rms(x, w):
    return x * lax.rsqrt(jnp.mean(x * x, axis=-1, keepdims=True) + EPS) * w


def _inproj_kernel(x_ref, mod_ref, nw_ref, w_ref, y_ref, h_ref, *, tm, d, chunks_per_seq, ctx_chunks, batch):
    i, j = pl.program_id(0), pl.program_id(1)

    @pl.when(j == 0)
    def _():
        for s in range(tm // MOD_CHUNK):
            row = _mod_row_index(i * (tm // MOD_CHUNK) + s, chunks_per_seq, ctx_chunks, batch)
            shift = mod_ref[pl.ds(row, 1), 0:d]
            scale = mod_ref[pl.ds(row, 1), d:2 * d]
            xs = x_ref[s * MOD_CHUNK:(s + 1) * MOD_CHUNK, :]
            h = _rms(xs, nw_ref[...]) * (1.0 + scale) + shift
            h_ref[s * MOD_CHUNK:(s + 1) * MOD_CHUNK, :] = h.astype(BF16)

    y_ref[...] = _dot(h_ref[...], w_ref[...])


def _inproj(x2d, mod_l, norm_w, w_in_bf, layer, *, chunks_per_seq, ctx_chunks, batch):
    r, d = x2d.shape
    tm, tn = 1024, 1024
    assert r % tm == 0 and N_PROJ % tn == 0
    kern = functools.partial(_inproj_kernel, tm=tm, d=d, chunks_per_seq=chunks_per_seq,
                             ctx_chunks=ctx_chunks, batch=batch)
    return pl.pallas_call(
        kern,
        out_shape=(jax.ShapeDtypeStruct((r, N_PROJ), F32), jax.ShapeDtypeStruct((r, d), BF16)),
        grid=(r // tm, N_PROJ // tn),
        in_specs=[
            pl.BlockSpec((tm, d), lambda i, j: (i, 0)),
            pl.BlockSpec(mod_l.shape, lambda i, j: (0, 0)),
            pl.BlockSpec((None, 1, d), lambda i, j: (layer, 0, 0)),
            pl.BlockSpec((None, d, tn), lambda i, j: (layer, 0, j)),
        ],
        out_specs=(pl.BlockSpec((tm, tn), lambda i, j: (i, j)),
                   pl.BlockSpec((tm, d), lambda i, j: (i, 0))),
        compiler_params=_cparams(("arbitrary", "arbitrary"), 52),
        name="inproj",
    )(x2d, mod_l, norm_w, w_in_bf)


def _hg_chunk(q, k, v, cum, st, keep, reverse):
    c = q.shape[0]
    tot = cum[0:1] if reverse else cum[c - 1:c]
    qe = (q * jnp.exp(cum)).astype(BF16)
    o_inter = _dot_nt(qe, st.astype(BF16))
    kd = k * jnp.exp(tot - cum)
    st_new = st * jnp.exp(tot) + _dot_tn(v.astype(BF16), kd.astype(BF16))

    rows = []
    for j in range(c // HG_BLK):
        lo = j * HG_BLK
        mid = lo + HG_BLK // 2 - (0 if reverse else 1)
        r = cum[mid:mid + 1]
        qt = (q[lo:lo + HG_BLK] * jnp.exp(cum[lo:lo + HG_BLK] - r)).astype(BF16)
        kh = (k * jnp.exp(jnp.minimum(r - cum, HG_EXP_CAP))).astype(BF16)
        rows.append(_dot_nt(qt, kh))
    a = jnp.where(keep, jnp.concatenate(rows, axis=0), 0.0)
    return _dot(a.astype(BF16), v.astype(BF16)) + o_inter, st_new


def _hgrn2_kernel(q_ref, ff_ref, fb_ref, i_ref, g_ref, lb_ref, nw_ref, o_ref, accf_ref, accb_ref, *, ctx, lt):
    nchunks = lt // HG_CHUNK
    nctx = ctx // HG_CHUNK
    dirs = []
    for direction, (f_ref, acc_ref) in enumerate(((ff_ref, accf_ref), (fb_ref, accb_ref))):
        keep, tri = _tri_matrix(HG_CHUNK, direction == 1)
        dirs.append((direction == 1, f_ref, acc_ref, keep, tri, lb_ref[direction:direction + 1, :]))

    def body(step, states):
        new_states = []
        for (reverse, f_ref, acc_ref, keep, tri, lbd), st in zip(dirs, states):
            ci = _scan_chunk(step, nchunks, nctx, reverse)
            rows = pl.ds(pl.multiple_of(ci * HG_CHUNK, HG_CHUNK), HG_CHUNK)
            q = _silu(q_ref[rows, :]) * (HG_DK ** -0.5)
            f = lbd + (1.0 - lbd) * _sigmoid(f_ref[rows, :])
            logf = jnp.log(jnp.maximum(f, MIN_FORGET))
            k = 1.0 - f
            v = i_ref[rows, :]
            cum = _tri_cumsum(tri, logf)
            o, st_new = _hg_chunk(q, k, v, cum, st, keep, reverse)
            acc_ref[rows, :] = o
            new_states.append(st_new)
        return tuple(new_states)

    zero = jnp.zeros((HG_DK, HG_DK), F32)
    lax.fori_loop(0, nchunks, body, (zero, zero), unroll=2)

    y = accf_ref[...] + accb_ref[...]
    o_ref[...] = (_rms(y, nw_ref[...]) * _silu(g_ref[...])).astype(o_ref.dtype)


def _hgrn2(y3, lb, norm_w, layer, ctx):
    b, lt, _ = y3.shape
    blk = lambda off: pl.BlockSpec((None, lt, HG_DK), lambda bi, h, off=off: (bi, 0, off + h))
    return pl.pallas_call(
        functools.partial(_hgrn2_kernel, ctx=ctx, lt=lt),
        out_shape=jax.ShapeDtypeStruct((b, lt, HG_WIDTH), BF16),
        grid=(b, HG_HEADS),
        in_specs=[blk(0), blk(HG_HEADS), blk(2 * HG_HEADS), blk(3 * HG_HEADS), blk(4 * HG_HEADS),
                  pl.BlockSpec((2, HG_DK), lambda bi, h: (0, h)),
                  pl.BlockSpec((None, 1, HG_DK), lambda bi, h: (layer, 0, 0))],
        out_specs=pl.BlockSpec((None, lt, HG_DK), lambda bi, h: (bi, 0, h)),
        scratch_shapes=[pltpu.VMEM((lt, HG_DK), F32)] * 2,
        compiler_params=_cparams(("arbitrary", "arbitrary"), 40),
        name="hgrn2",
    )(y3, y3, y3, y3, y3, lb, norm_w)


def _na_kernel(q_ref, k_ref, v_ref, bias_ref, qw_ref, kw_ref, o_ref, qs_ref, ks_ref, vs_ref, *, ctx, seq):
    rows = seq // GRID_W
    wr = min(NA_WIN_ROWS, rows)
    scale = NA_HD ** -0.5
    qs_ref[...] = (_rms(q_ref[...], qw_ref[...]) * scale).astype(BF16)
    ks_ref[...] = _rms(k_ref[...], kw_ref[...]).astype(BF16)
    vs_ref[...] = v_ref[...].astype(BF16)
    kc = ks_ref[0:ctx, :]
    vc = vs_ref[0:ctx, :]

    s = _dot_nt(qs_ref[0:ctx, :], kc)
    p = jnp.exp(s - jnp.max(s, axis=-1, keepdims=True))
    o = _dot(p.astype(BF16), vc) / jnp.sum(p, axis=-1, keepdims=True)
    o_ref[0:ctx, :] = o.astype(o_ref.dtype)

    def body(r, carry):
        r0 = jnp.clip(r - wr // 2, 0, rows - wr)
        qrow = pl.ds(pl.multiple_of(ctx + r * GRID_W, GRID_W), GRID_W)
        krow = pl.ds(pl.multiple_of(ctx + r0 * GRID_W, GRID_W), wr * GRID_W)
        qr = qs_ref[qrow, :]
        s_lat = _dot_nt(qr, ks_ref[krow, :]) + bias_ref[r0 - r + NA_WIN_ROWS - 1]
        s_ctx = _dot_nt(qr, kc)
        m = jnp.maximum(jnp.max(s_lat, axis=-1, keepdims=True), jnp.max(s_ctx, axis=-1, keepdims=True))
        p_lat = jnp.exp(s_lat - m)
        p_ctx = jnp.exp(s_ctx - m)
        den = jnp.sum(p_lat, axis=-1, keepdims=True) + jnp.sum(p_ctx, axis=-1, keepdims=True)
        o = (_dot(p_lat.astype(BF16), vs_ref[krow, :]) + _dot(p_ctx.astype(BF16), vc)) / den
        o_ref[qrow, :] = o.astype(o_ref.dtype)
        return carry

    lax.fori_loop(0, rows, body, 0)


def _na_bias_table(rpb, rows):
    wr = min(NA_WIN_ROWS, rows)
    j = np.arange(GRID_W)
    c0 = np.clip(j - NA_WIN_COLS // 2, 0, GRID_W - NA_WIN_COLS)
    col_mask = (j[None, :] >= c0[:, None]) & (j[None, :] < c0[:, None] + NA_WIN_COLS)
    dc = np.clip(j[None, :] - j[:, None], -(NA_WIN_COLS - 1), NA_WIN_COLS - 1) + NA_WIN_COLS - 1
    onehot = jnp.asarray(dc[:, :, None] == np.arange(2 * NA_WIN_COLS - 1)[None, None, :], F32)
    base = jnp.einsum("hrd,qcd->hrqc", rpb.astype(F32), onehot, precision=lax.Precision.HIGHEST)
    base = jnp.where(col_mask[None, None], base, NEG_BIG)
    t = jnp.stack([base[:, i:i + wr] for i in range(NA_WIN_ROWS)], axis=1)
    return t.transpose(0, 1, 3, 2, 4).reshape(rpb.shape[0], NA_WIN_ROWS, GRID_W, wr * GRID_W)


def _nattn(y3, bias_tab, qw, kw, layer, ctx):
    b, lt, _ = y3.shape
    seq = lt - ctx
    base = C_NA // NA_HD
    blk = lambda off: pl.BlockSpec((None, lt, NA_HD), lambda bi, h, off=off: (bi, 0, base + off + h))
    return pl.pallas_call(
        functools.partial(_na_kernel, ctx=ctx, seq=seq),
        out_shape=jax.ShapeDtypeStruct((b, lt, NA_WIDTH), BF16),
        grid=(b, NA_HEADS),
        in_specs=[blk(0), blk(NA_HEADS), blk(2 * NA_HEADS),
                  pl.BlockSpec((None,) + bias_tab.shape[1:], lambda bi, h: (h, 0, 0, 0)),
                  pl.BlockSpec((None, 1, NA_HD), lambda bi, h: (layer, 0, 0)),
                  pl.BlockSpec((None, 1, NA_HD), lambda bi, h: (layer, 0, 0))],
        out_specs=pl.BlockSpec((None, lt, NA_HD), lambda bi, h: (bi, 0, h)),
        scratch_shapes=[pltpu.VMEM((lt, NA_HD), BF16)] * 3,
        compiler_params=_cparams(("arbitrary", "arbitrary"), 40),
        name="nattn",
    )(y3, y3, y3, bias_tab, qw, kw)


def _rope_tables(seq):
    half = SSD_STATE // 2
    pos = np.arange(seq)
    inv_freq = 1.0 / (ROPE_BASE ** (np.arange(0, half, 2, dtype=np.float64) / half))
    lane = np.arange(SSD_STATE)
    p = np.where(lane[None, :] < half, (pos // GRID_W)[:, None], (pos % GRID_W)[:, None])
    ang = p * inv_freq[lane % (half // 2)][None, :]
    sign = np.where((lane % half) < half // 2, -1.0, 1.0)
    return jnp.asarray(np.cos(ang), F32), jnp.asarray(np.sin(ang) * sign[None, :], F32)


def _conv_silu(x, w, bias, ctx, lt):
    t = lax.broadcasted_iota(jnp.int32, (lt, 1), 0)
    pos = jnp.where(t < ctx, t, t - ctx)
    length = jnp.where(t < ctx, ctx, lt - ctx)
    left = (SSD_CONV - 1) // 2
    acc = bias + jnp.zeros_like(x)
    for j in range(SSD_CONV):
        off = j - left
        xs = x if off == 0 else pltpu.roll(x, (-off) % lt, axis=0)
        ok = (pos + off >= 0) & (pos + off < length)
        acc = acc + jnp.where(ok, xs, 0.0) * w[j:j + 1, :]
    return _silu(acc)


def _rope(t, cos, sin_signed):
    lane = lax.broadcasted_iota(jnp.int32, t.shape, 1)
    quarter = SSD_STATE // 4
    swapped = jnp.where((lane & quarter) == 0, pltpu.roll(t, SSD_STATE - quarter, axis=1),
                        pltpu.roll(t, quarter, axis=1))
    return t * cos + swapped * sin_signed


def _ssd_kernel(x_ref, b0_ref, b1_ref, c0_ref, c1_ref, dt_ref, wx_ref, wb0_ref, wb1_ref, wc0_ref, wc1_ref,
                bx_ref, bb0_ref, bb1_ref, bc0_ref, bc1_ref, dtb_ref, alog_ref, dskip_ref, cos_ref, sin_ref,
                o_ref, xs_ref, bs0_ref, bs1_ref, cs0_ref, cs1_ref, dts_ref, yb_ref, cumc_ref, cumt_ref, dtt_ref,
                bt0_ref, bt1_ref, *, ctx, lt):
    nchunks = lt // SSD_CHUNK
    nctx = ctx // SSD_CHUNK
    pair = pl.program_id(1)

    xs_ref[...] = _conv_silu(x_ref[...], wx_ref[...], bx_ref[...], ctx, lt)
    for src, w, bb, dst in ((b0_ref, wb0_ref, bb0_ref, bs0_ref), (b1_ref, wb1_ref, bb1_ref, bs1_ref),
                            (c0_ref, wc0_ref, bc0_ref, cs0_ref), (c1_ref, wc1_ref, bc1_ref, cs1_ref)):
        t = _conv_silu(src[...], w[...], bb[...], ctx, lt)
        dst[0:ctx, :] = t[0:ctx].astype(BF16)
        dst[ctx:lt, :] = _rope(t[ctx:lt], cos_ref[...], sin_ref[...]).astype(BF16)
    raw = dt_ref[...] + dtb_ref[...]
    dts_ref[...] = jnp.maximum(raw, 0.0) + jnp.log(1.0 + jnp.exp(-jnp.abs(raw)))

    lane = lax.broadcasted_iota(jnp.int32, (SSD_CHUNK, LANES), 1)
    a_neg = -jnp.exp(alog_ref[...])
    masks = [_tri_matrix(SSD_CHUNK, reverse) for reverse in (False, True)]

    def prep(ci, carry):
        rows = pl.ds(pl.multiple_of(ci * SSD_CHUNK, SSD_CHUNK), SSD_CHUNK)
        dt_all = dts_ref[rows, :]
        dtt_ref[ci] = dt_all.T
        for direction in range(2):
            cum = _tri_cumsum(masks[direction][1], dt_all * a_neg)
            cumc_ref[direction, rows, :] = cum
            cumt_ref[direction, ci] = cum.T
        bt0_ref[ci] = bs0_ref[rows, :].astype(F32).T.astype(BF16)
        bt1_ref[ci] = bs1_ref[rows, :].astype(F32).T.astype(BF16)
        return carry

    lax.fori_loop(0, nchunks, prep, 0, unroll=2)

    def body(step, states):
        new_states = []
        for direction in range(2):
            reverse = direction == 1
            keep, tri = masks[direction]
            ci = _scan_chunk(step, nchunks, nctx, reverse)
            rows = pl.ds(pl.multiple_of(ci * SSD_CHUNK, SSD_CHUNK), SSD_CHUNK)
            x = xs_ref[rows, :]
            dt_all = dts_ref[rows, :]
            cum_all = cumc_ref[direction, rows, :]
            y = jnp.zeros((SSD_CHUNK, LANES), F32)
            for hh, (bt_ref, cs_ref) in enumerate(((bt0_ref, cs0_ref), (bt1_ref, cs1_ref))):
                chan = direction * SSD_HEADS + 2 * pair + hh
                sel = lane == chan
                dt_c = jnp.sum(jnp.where(sel, dt_all, 0.0), axis=-1, keepdims=True)
                cum_c = jnp.sum(jnp.where(sel, cum_all, 0.0), axis=-1, keepdims=True)
                dt_cb = jnp.broadcast_to(dt_c, (SSD_CHUNK, LANES))
                cum_cb = jnp.broadcast_to(cum_c, (SSD_CHUNK, LANES))
                cum_rb = cumt_ref[direction, ci, pl.ds(chan, 1), :]
                dt_rb = dtt_ref[ci, pl.ds(chan, 1), :]
                tot = cum_cb[0:1] if reverse else cum_cb[SSD_CHUNK - 1:SSD_CHUNK]
                seg = jnp.where(keep, jnp.exp(jnp.where(keep, cum_cb - cum_rb, 0.0)), 0.0)
                bt = bt_ref[ci]
                cm = cs_ref[rows, :]
                m = _dot(cm, bt) * seg * dt_rb
                half_mask = (lane // SSD_HD) == hh
                xm = jnp.where(half_mask, x, 0.0)
                st = states[2 * direction + hh]
                yh = _dot(m.astype(BF16), xm.astype(BF16)) + _dot(cm, st.astype(BF16)) * jnp.exp(cum_cb)
                w_end = dt_cb * jnp.exp(tot - cum_cb)
                st_new = st * jnp.exp(tot) + _dot(bt, (xm * w_end).astype(BF16))
                y = y + yh
                new_states.append(st_new)
            if direction == 0:
                o_ref[rows, :] = y + dskip_ref[...] * x
            else:
                yb_ref[rows, :] = y
        return tuple(new_states)

    zero = jnp.zeros((SSD_STATE, LANES), F32)
    lax.fori_loop(0, nchunks, body, (zero,) * 4, unroll=2)
    o_ref[...] += yb_ref[...]


def _ssd(y3, conv_w, conv_b, dt_bias_l, a_log_l, d_lanes, cos_t, sin_t, layer, ctx):
    b, lt, _ = y3.shape
    npair = SSD_HEADS // 2
    hg = SSD_HEADS // SSD_GROUPS
    xb = C_XBC // LANES
    bb = xb + SSD_WIDTH // LANES
    cb = bb + SSD_GROUPS
    g0 = lambda p: (2 * p) // hg
    g1 = lambda p: (2 * p + 1) // hg
    yblk = lambda f: pl.BlockSpec((None, lt, LANES), lambda bi, p, f=f: (bi, 0, f(p)))
    wblk = lambda f: pl.BlockSpec((None, SSD_CONV, LANES), lambda bi, p, f=f: (layer, 0, f(p)))
    bblk = lambda f: pl.BlockSpec((None, 1, LANES), lambda bi, p, f=f: (layer, 0, f(p)))
    cols = [lambda p: p, lambda p: npair + g0(p), lambda p: npair + g1(p),
            lambda p: npair + SSD_GROUPS + g0(p), lambda p: npair + SSD_GROUPS + g1(p)]
    const2 = lambda shape: pl.BlockSpec(shape, lambda bi, p: (0, 0))
    return pl.pallas_call(
        functools.partial(_ssd_kernel, ctx=ctx, lt=lt),
        out_shape=jax.ShapeDtypeStruct((b, lt, SSD_WIDTH), F32),
        grid=(b, npair),
        in_specs=[yblk(lambda p: xb + p), yblk(lambda p: bb + g0(p)), yblk(lambda p: bb + g1(p)),
                  yblk(lambda p: cb + g0(p)), yblk(lambda p: cb + g1(p)), yblk(lambda p: C_DT // LANES)]
                 + [wblk(f) for f in cols] + [bblk(f) for f in cols]
                 + [const2((1, LANES)), const2((1, LANES)),
                    pl.BlockSpec((None, 1, LANES), lambda bi, p: (layer, 0, p)),
                    const2(cos_t.shape), const2(sin_t.shape)],
        out_specs=pl.BlockSpec((None, lt, LANES), lambda bi, p: (bi, 0, p)),
        scratch_shapes=[pltpu.VMEM((lt, LANES), F32)] + [pltpu.VMEM((lt, LANES), BF16)] * 4
                       + [pltpu.VMEM((lt, LANES), F32)] * 2
                       + [pltpu.VMEM((2, lt, LANES), F32),
                          pltpu.VMEM((2, lt // SSD_CHUNK, SSD_CHUNK, LANES), F32),
                          pltpu.VMEM((lt // SSD_CHUNK, SSD_CHUNK, LANES), F32)]
                       + [pltpu.VMEM((lt // SSD_CHUNK, SSD_STATE, SSD_CHUNK), BF16)] * 2,
        compiler_params=_cparams(("arbitrary", "arbitrary"), 48),
        name="ssd",
    )(y3, y3, y3, y3, y3, y3, *([conv_w] * 5), *([conv_b] * 5), dt_bias_l, a_log_l, d_lanes, cos_t, sin_t)


def _mix_kernel(h_ref, yh_ref, yn_ref, ys_ref, z_ref, snw_ref, wh_ref, wn_ref, ws_ref, g0_ref, g1_ref, g2_ref,
                m_ref, ysn_ref):
    @pl.when(pl.program_id(1) == 0)
    def _():
        y = ys_ref[...] * _silu(z_ref[...])
        gw = SSD_WIDTH // SSD_GROUPS
        lane = lax.broadcasted_iota(jnp.int32, y.shape, 1)
        inv = jnp.zeros_like(y)
        for g in range(SSD_GROUPS):
            in_g = (lane >= g * gw) & (lane < (g + 1) * gw)
            ms = jnp.sum(jnp.where(in_g, y * y, 0.0), axis=-1, keepdims=True) * (1.0 / gw)
            inv = jnp.where(in_g, lax.rsqrt(ms + EPS), inv)
        ysn_ref[...] = (y * inv * snw_ref[...]).astype(BF16)

    h = h_ref[...]
    m = _sigmoid(_dot(h, g0_ref[...])) * _dot(yh_ref[...], wh_ref[...])
    m = m + _sigmoid(_dot(h, g1_ref[...])) * _dot(yn_ref[...], wn_ref[...])
    m = m + _sigmoid(_dot(h, g2_ref[...])) * _dot(ysn_ref[...], ws_ref[...])
    m_ref[...] = m.astype(m_ref.dtype)


def _mix(h, y_hg, y_na, y_ssd, y2d, ssd_nw, wb_hg, wb_na, wb_ssd, w_gate_bf, layer):
    r, d = h.shape
    tm, tn = 512, 512
    nt = d // tn
    row = lambda w: pl.BlockSpec((tm, w), lambda i, j: (i, 0))
    wspec = lambda k: pl.BlockSpec((None, k, tn), lambda i, j: (layer, 0, j))
    gspec = lambda br: pl.BlockSpec((None, d, tn), lambda i, j, br=br: (layer, 0, br * nt + j))
    return pl.pallas_call(
        _mix_kernel,
        out_shape=jax.ShapeDtypeStruct((r, d), BF16),
        grid=(r // tm, nt),
        in_specs=[row(d), row(HG_WIDTH), row(NA_WIDTH), row(SSD_WIDTH),
                  pl.BlockSpec((tm, SSD_WIDTH), lambda i, j: (i, C_Z // SSD_WIDTH)),
                  pl.BlockSpec((None, 1, SSD_WIDTH), lambda i, j: (layer, 0, 0)),
                  wspec(HG_WIDTH), wspec(NA_WIDTH), wspec(SSD_WIDTH), gspec(0), gspec(1), gspec(2)],
        out_specs=pl.BlockSpec((tm, tn), lambda i, j: (i, j)),
        scratch_shapes=[pltpu.VMEM((tm, SSD_WIDTH), BF16)],
        compiler_params=_cparams(("arbitrary", "arbitrary"), 48),
        name="mix",
    )(h, y_hg, y_na, y_ssd, y2d, ssd_nw, wb_hg, wb_na, wb_ssd, w_gate_bf, w_gate_bf, w_gate_bf)


def _outproj_kernel(x_ref, m_ref, mod_ref, nw_ref, wo_ref, wr_ref, br_ref, xo_ref, h2_ref, lg_ref,
                    *, d, chunks_per_seq, ctx_chunks, batch):
    row = _mod_row_index(pl.program_id(0), chunks_per_seq, ctx_chunks, batch)
    g1 = mod_ref[pl.ds(row, 1), 2 * d:3 * d]
    sh2 = mod_ref[pl.ds(row, 1), 3 * d:4 * d]
    sc2 = mod_ref[pl.ds(row, 1), 4 * d:5 * d]
    x = x_ref[...] + g1 * _dot(m_ref[...], wo_ref[...])
    xo_ref[...] = x
    h2 = _rms(x, nw_ref[...]) * (1.0 + sc2) + sh2
    h2_ref[...] = h2
    a, b, _ = _split3(h2)
    wa, wb, _ = _split3(wr_ref[...])
    lg_ref[...] = _dot(a, wa) + _dot(a, wb) + _dot(b, wa) + br_ref[...]


def _outproj(x2d, m, mod_l, norm_w, w_out_bf, w_router, b_router, layer, *, chunks_per_seq, ctx_chunks, batch):
    r, d = x2d.shape
    tm = MOD_CHUNK
    kern = functools.partial(_outproj_kernel, d=d, chunks_per_seq=chunks_per_seq, ctx_chunks=ctx_chunks,
                             batch=batch)
    return pl.pallas_call(
        kern,
        out_shape=(jax.ShapeDtypeStruct((r, d), F32), jax.ShapeDtypeStruct((r, d), F32),
                   jax.ShapeDtypeStruct((r, LANES), F32)),
        grid=(r // tm,),
        in_specs=[pl.BlockSpec((tm, d), lambda i: (i, 0)),
                  pl.BlockSpec((tm, d), lambda i: (i, 0)),
                  pl.BlockSpec(mod_l.shape, lambda i: (0, 0)),
                  pl.BlockSpec((None, 1, d), lambda i: (layer, 0, 0)),
                  pl.BlockSpec((None, d, d), lambda i: (layer, 0, 0)),
                  pl.BlockSpec((None, d, LANES), lambda i: (layer, 0, 0)),
                  pl.BlockSpec((None, 1, LANES), lambda i: (layer, 0, 0))],
        out_specs=(pl.BlockSpec((tm, d), lambda i: (i, 0)), pl.BlockSpec((tm, d), lambda i: (i, 0)),
                   pl.BlockSpec((tm, LANES), lambda i: (i, 0))),
        compiler_params=_cparams(("arbitrary",), 48),
        name="outproj",
    )(x2d, m, mod_l, norm_w, w_out_bf, w_router, b_router)


def _moe_kernel(ie_ref, ib_ref, in_ref, if_ref, x_hbm, wg_ref, wu_ref, wd_ref, bg_ref, bu_ref, bd_ref, y_hbm,
                xbuf, acc, sem_in, sem_out, *, nj, n_items):
    it, j = pl.program_id(0), pl.program_id(1)
    nblk = in_ref[it]
    nfill = if_ref[it]
    row0 = pl.multiple_of(ib_ref[it] * MOE_RB, MOE_RB)

    def in_copy(c):
        return pltpu.make_async_copy(x_hbm.at[pl.ds(row0 + c * MOE_RB, MOE_RB), :],
                                     xbuf.at[pl.ds(c * MOE_RB, MOE_RB), :], sem_in)

    def out_copy(c, src_block):
        return pltpu.make_async_copy(acc.at[pl.ds(src_block * MOE_RB, MOE_RB), :],
                                     y_hbm.at[pl.ds(row0 + c * MOE_RB, MOE_RB), :], sem_out)

    def wait_out(n):
        one = pltpu.make_async_copy(acc.at[pl.ds(0, MOE_RB), :], y_hbm.at[pl.ds(0, MOE_RB), :], sem_out)
        lax.fori_loop(0, n, lambda c, _: (one.wait(), 0)[1], 0)

    n_prev = jnp.where(it > 0, in_ref[jnp.maximum(it - 1, 0)], 0)

    @pl.when((it == 0) & (j == 0))
    def _():
        acc[...] = jnp.zeros(acc.shape, F32)

    @pl.when((j == 0) & (nblk == 0))
    def _():
        wait_out(n_prev)

    @pl.when((nfill > 0) & (j == 0))
    def _():
        acc[0:MOE_RB, :] = jnp.zeros((MOE_RB, acc.shape[1]), F32)
        lax.fori_loop(0, nfill, lambda c, _: (out_copy(c, 0).start(), 0)[1], 0)
        wait_out(nfill)

    @pl.when(nblk > 0)
    def _():
        @pl.when(j == 0)
        def _():
            lax.fori_loop(0, nblk, lambda c, _: (in_copy(c).start(), 0)[1], 0)
            lax.fori_loop(0, nblk, lambda c, _: (in_copy(c).wait(), 0)[1], 0)

        def block(c, weights):
            wg, wu, wd = weights
            start = c * MOE_RB if isinstance(c, int) else pl.multiple_of(c * MOE_RB, MOE_RB)
            rows = pl.ds(start, MOE_RB)
            xb = xbuf[rows, :].astype(BF16)
            gate = jnp.minimum(_dot(xb, wg) + bg_ref[...], SWIGLU_LIMIT)
            up = jnp.clip(_dot(xb, wu) + bu_ref[...], -SWIGLU_LIMIT, SWIGLU_LIMIT)
            a = gate * _sigmoid(SWIGLU_ALPHA * gate) * (up + 1.0)
            part = _dot(a.astype(BF16), wd)

            if not isinstance(c, int):
                pl.when((j == 0) & (c == 0))(lambda: wait_out(n_prev))
            elif c == 0:
                pl.when(j == 0)(lambda: wait_out(n_prev))

            base = jnp.where(j == 0, jnp.broadcast_to(bd_ref[...], part.shape), acc[rows, :])
            acc[rows, :] = base + part

        def cast_weights():
            return wg_ref[...].astype(BF16), wu_ref[...].astype(BF16), wd_ref[...].astype(BF16)

        @pl.when(nblk == MOE_KB)
        def _():
            weights = cast_weights()
            for c in range(MOE_KB):
                block(c, weights)

        @pl.when(nblk < MOE_KB)
        def _():
            weights = cast_weights()
            lax.fori_loop(0, nblk, lambda c, _: (block(c, weights), 0)[1], 0)

        @pl.when(j == nj - 1)
        def _():
            lax.fori_loop(0, nblk, lambda c, _: (out_copy(c, c).start(), 0)[1], 0)

            @pl.when(it == n_items - 1)
            def _():
                wait_out(nblk)


def _moe_experts(item_e, item_blk, item_n, item_fill, x_sorted, w_gate, b_gate, w_up, b_up, w_down, b_down, layer):
    cap, d = x_sorted.shape
    f = w_gate.shape[-1]
    nj = f // MOE_TF
    n_items = item_e.shape[0]
    depth, ne = b_gate.shape[:2]

    def jsel(it, j, n):
        return jnp.where(n[it] > 0, j, nj - 1)

    w_in = lambda: pl.BlockSpec((None, None, d, MOE_TF), lambda it, j, e, b, n, fl: (layer, e[it], 0, jsel(it, j, n)))
    b_in = lambda: pl.BlockSpec((None, None, 1, MOE_TF), lambda it, j, e, b, n, fl: (layer, e[it], 0, jsel(it, j, n)))
    grid_spec = pltpu.PrefetchScalarGridSpec(
        num_scalar_prefetch=4,
        grid=(n_items, nj),
        in_specs=[pl.BlockSpec(memory_space=pl.ANY), w_in(), w_in(),
                  pl.BlockSpec((None, None, MOE_TF, d), lambda it, j, e, b, n, fl: (layer, e[it], jsel(it, j, n), 0)),
                  b_in(), b_in(),
                  pl.BlockSpec((None, None, 1, d), lambda it, j, e, b, n, fl: (layer, e[it], 0, 0))],
        out_specs=pl.BlockSpec(memory_space=pl.ANY),
        scratch_shapes=[pltpu.VMEM((MOE_KB * MOE_RB, d), F32), pltpu.VMEM((MOE_KB * MOE_RB, d), F32),
                        pltpu.SemaphoreType.DMA, pltpu.SemaphoreType.DMA],
    )
    return pl.pallas_call(
        functools.partial(_moe_kernel, nj=nj, n_items=n_items),
        out_shape=jax.ShapeDtypeStruct((cap, d), F32),
        grid_spec=grid_spec,
        compiler_params=_cparams(("arbitrary", "arbitrary"), 60),
        name="moe_experts",
    )(item_e, item_blk, item_n, item_fill, x_sorted, w_gate, w_up, w_down,
      b_gate.reshape(depth, ne, 1, f), b_up.reshape(depth, ne, 1, f), b_down.reshape(depth, ne, 1, d))


def _combine_kernel(x_ref, y_ref, w_ref, mod_ref, o_ref, *, d, chunks_per_seq, ctx_chunks, batch):
    row = _mod_row_index(pl.program_id(0), chunks_per_seq, ctx_chunks, batch)
    g2 = mod_ref[pl.ds(row, 1), 5 * d:6 * d]
    w = w_ref[...]
    f = y_ref[0] * w[:, 0:1]
    for k in range(1, TOP_K):
        f = f + y_ref[k] * w[:, k:k + 1]
    o_ref[...] = x_ref[...] + g2 * f


def _combine(x2d, yk, top_w, mod_l, *, chunks_per_seq, ctx_chunks, batch):
    r, d = x2d.shape
    tm = MOD_CHUNK
    kern = functools.partial(_combine_kernel, d=d, chunks_per_seq=chunks_per_seq, ctx_chunks=ctx_chunks,
                             batch=batch)
    return pl.pallas_call(
        kern,
        out_shape=jax.ShapeDtypeStruct((r, d), F32),
        grid=(r // tm,),
        in_specs=[pl.BlockSpec((tm, d), lambda i: (i, 0)),
                  pl.BlockSpec((TOP_K, tm, d), lambda i: (0, i, 0)),
                  pl.BlockSpec((tm, TOP_K), lambda i: (i, 0)),
                  pl.BlockSpec(mod_l.shape, lambda i: (0, 0))],
        out_specs=pl.BlockSpec((tm, d), lambda i: (i, 0)),
        compiler_params=_cparams(("arbitrary",), 40),
        name="combine",
    )(x2d, yk, top_w, mod_l)


def _moe_plan(logits, n_tok):
    top_val, top_idx = lax.top_k(logits, TOP_K)
    top_w = jax.nn.softmax(top_val, axis=-1)
    n_assign = n_tok * TOP_K
    exp_id = top_idx.reshape(-1)
    onehot = (exp_id[:, None] == jnp.arange(N_EXPERTS)[None, :]).astype(jnp.int32)
    csum = jnp.cumsum(onehot, axis=0)
    rank = jnp.sum((csum - onehot) * onehot, axis=1)
    counts = csum[-1]
    nblk = (counts + MOE_RB - 1) // MOE_RB
    blk_end = jnp.cumsum(nblk)
    blk_start = blk_end - nblk
    dest = blk_start[exp_id] * MOE_RB + rank
    total_blocks = (n_assign + N_EXPERTS * (MOE_RB - 1)) // MOE_RB
    cap = total_blocks * MOE_RB
    slot_tok = jnp.zeros((cap,), jnp.int32).at[dest].set(jnp.arange(n_assign, dtype=jnp.int32) // TOP_K)

    n_items = N_EXPERTS + total_blocks // MOE_KB
    items_e = (nblk + MOE_KB - 1) // MOE_KB
    item_end = jnp.cumsum(items_e)
    item_start = item_end - items_e
    idx = jnp.arange(n_items)
    e_of = jnp.minimum(jnp.searchsorted(item_end, idx, side="right"), N_EXPERTS - 1).astype(jnp.int32)
    valid = idx < item_end[-1]
    e_last = e_of[jnp.maximum(item_end[-1] - 1, 0)]
    kk = idx - item_start[e_of]
    item_e = jnp.where(valid, e_of, e_last).astype(jnp.int32)
    fill_blk = blk_end[-1] + (idx - item_end[-1]) * MOE_KB
    item_blk = jnp.where(valid, blk_start[e_of] + kk * MOE_KB, jnp.minimum(fill_blk, total_blocks)).astype(jnp.int32)
    item_n = jnp.where(valid, jnp.clip(nblk[e_of] - kk * MOE_KB, 0, MOE_KB), 0).astype(jnp.int32)
    item_fill = jnp.where(valid, 0, jnp.clip(total_blocks - fill_blk, 0, MOE_KB)).astype(jnp.int32)
    return top_w, dest.reshape(n_tok, TOP_K), slot_tok, item_e, item_blk, item_n, item_fill


def kernel(x, c, ctx, c_ctx, w_mod, b_mod, norm1_w, norm2_w, w_in, hg_lb_logits, hg_norm_w, na_q_norm_w,
           na_k_norm_w, na_rpb, ssd_conv_w, ssd_conv_b, ssd_dt_bias, ssd_a_log, ssd_d, ssd_norm_w, w_branch_hg,
           w_branch_na, w_branch_ssd, w_out, moe_w_router, moe_b_router, moe_w_gate, moe_b_gate, moe_w_up,
           moe_b_up, moe_w_down, moe_b_down):
    batch, seq, d = x.shape
    n_ctx = ctx.shape[1]
    depth = w_mod.shape[0]
    lt = n_ctx + seq
    r = batch * lt
    chunks = dict(chunks_per_seq=lt // MOD_CHUNK, ctx_chunks=n_ctx // MOD_CHUNK, batch=batch)
    assert n_ctx % MOD_CHUNK == 0 and seq % MOD_CHUNK == 0 and batch < SUBLANES

    lb_soft = jax.nn.softmax(hg_lb_logits.astype(F32), axis=0)
    lower_bounds = jnp.cumsum(lb_soft, axis=0) - lb_soft[0]
    w_proj_bf = w_in[:, :, :N_PROJ].astype(BF16)
    w_gate_bf = w_in[:, :, N_MIX_COLS:].astype(BF16)
    wb_hg, wb_na, wb_ssd = (w.astype(BF16) for w in (w_branch_hg, w_branch_na, w_branch_ssd))
    w_out_bf = w_out.astype(BF16)
    w_router = jnp.pad(moe_w_router, ((0, 0), (0, 0), (0, LANES - N_EXPERTS)))
    b_router = jnp.pad(moe_b_router, ((0, 0), (0, LANES - N_EXPERTS))).reshape(depth, 1, LANES)
    vec = lambda p: p.reshape(depth, 1, p.shape[-1])
    dt_pad = lambda p: jnp.pad(p.reshape(depth, 2 * SSD_HEADS), ((0, 0), (0, LANES - 2 * SSD_HEADS)))
    dt_bias_p, a_log_p = dt_pad(ssd_dt_bias), dt_pad(ssd_a_log)
    d_lanes = jnp.repeat(ssd_d, SSD_HD, axis=-1).reshape(depth, 1, SSD_WIDTH)
    cos_t, sin_t = _rope_tables(seq)
    conv_b = vec(ssd_conv_b)

    cvecs = jnp.zeros((SUBLANES, d), F32).at[:batch].set(c).at[batch].set(c_ctx)
    mod = _mod_table(cvecs, w_mod, b_mod)

    xa = jnp.concatenate([ctx, x], axis=1).reshape(r, d)
    for layer in range(depth):
        mod_l = mod[layer]
        y2d, h = _inproj(xa, mod_l, vec(norm1_w), w_proj_bf, layer, **chunks)
        y3 = y2d.reshape(batch, lt, N_PROJ)
        y_hg = _hgrn2(y3, lower_bounds[layer], vec(hg_norm_w), layer, n_ctx)
        bias_tab = _na_bias_table(na_rpb[layer], seq // GRID_W)
        y_na = _nattn(y3, bias_tab, vec(na_q_norm_w), vec(na_k_norm_w), layer, n_ctx)
        y_ssd = _ssd(y3, ssd_conv_w, conv_b, dt_bias_p[layer:layer + 1], a_log_p[layer:layer + 1], d_lanes,
                     cos_t, sin_t, layer, n_ctx)
        m = _mix(h, y_hg.reshape(r, -1), y_na.reshape(r, -1), y_ssd.reshape(r, -1), y2d, vec(ssd_norm_w),
                 wb_hg, wb_na, wb_ssd, w_gate_bf, layer)
        xa, h2, logits = _outproj(xa, m, mod_l, vec(norm2_w), w_out_bf, w_router, b_router, layer, **chunks)

        top_w, dest, slot_tok, item_e, item_blk, item_n, item_fill = _moe_plan(logits[:, :N_EXPERTS], r)
        x_sorted = jnp.take(h2, slot_tok, axis=0, mode="clip")
        y_sorted = _moe_experts(item_e, item_blk, item_n, item_fill, x_sorted, moe_w_gate, moe_b_gate, moe_w_up, moe_b_up,
                                moe_w_down, moe_b_down, layer)
        yk = jnp.take(y_sorted, dest.T, axis=0, mode="clip")
        xa = _combine(xa, yk, top_w, mod_l, **chunks)
    return xa.reshape(batch, lt, d)[:, n_ctx:, :]
```

```python
import functools
import math

import numpy as np
import jax
import jax.numpy as jnp
from jax import lax
from jax.experimental import pallas as pl
from jax.experimental.pallas import tpu as pltpu

F32 = jnp.float32
BF16 = jnp.bfloat16

LANES = 128
SUBLANES = 8
VMEM_BYTES_V7X = 64 * 1024 * 1024

GRID_W = 64
EPS = 1e-6
NEG_BIG = -1e30
MIN_FORGET = 1e-6
HG_HEADS, HG_DK = 6, 128
HG_WIDTH = HG_HEADS * HG_DK
HG_CHUNK = 64
HG_BLK = 8
HG_EXP_CAP = 60.0
NA_HEADS, NA_HD = 4, 128
NA_WIDTH = NA_HEADS * NA_HD
NA_WIN_ROWS, NA_WIN_COLS = 8, 16
SSD_HEADS, SSD_HD = 12, 64
SSD_WIDTH = SSD_HEADS * SSD_HD
SSD_GROUPS, SSD_STATE, SSD_CONV = 4, 128, 4
SSD_CHUNK = 128
SSD_CONV_CH = SSD_WIDTH + 2 * SSD_GROUPS * SSD_STATE
ROPE_BASE = 10000.0
N_EXPERTS, TOP_K = 32, 4
SWIGLU_ALPHA, SWIGLU_LIMIT = 1.702, 7.0

C_HG = 0
C_NA = 5 * HG_WIDTH
C_Z = C_NA + 3 * NA_WIDTH
C_XBC = C_Z + SSD_WIDTH
C_DT = C_XBC + SSD_CONV_CH
N_MIX_COLS = C_DT + 2 * SSD_HEADS
N_PROJ = 8192

MOD_CHUNK = 256
MOE_RB = 256
MOE_KB = 5
MOE_TF = 512


def _cparams(sem, vmem_mb):
    return pltpu.CompilerParams(dimension_semantics=sem, vmem_limit_bytes=vmem_mb * 1024 * 1024)


def _dot(a, b):
    return jnp.dot(a, b, preferred_element_type=F32)


def _dot_nt(a, b):
    return lax.dot_general(a, b, (((1,), (1,)), ((), ())), preferred_element_type=F32)


def _dot_tn(a, b):
    return lax.dot_general(a, b, (((0,), (0,)), ((), ())), preferred_element_type=F32)


def _sigmoid(x):
    return 1.0 / (1.0 + jnp.exp(-x))


def _silu(x):
    return x * _sigmoid(x)


def _split3(x):
    a = x.astype(BF16)
    r = x - a.astype(F32)
    b = r.astype(BF16)
    c = (r - b.astype(F32)).astype(BF16)
    return a, b, c


def _tri_cumsum(tri, g):
    a, b, c = _split3(g)
    return _dot(tri, a) + _dot(tri, b) + _dot(tri, c)


def _scan_chunk(step, nchunks, nctx, reverse):
    if not reverse:
        return step
    return jnp.where(step < nctx, nctx - 1 - step, nchunks - 1 - (step - nctx))


def _tri_matrix(n, reverse):
    t = lax.broadcasted_iota(jnp.int32, (n, n), 0)
    s = lax.broadcasted_iota(jnp.int32, (n, n), 1)
    keep = (s >= t) if reverse else (s <= t)
    return keep, jnp.where(keep, 1.0, 0.0).astype(BF16)


def _mod_kernel(c_ref, w_ref, b_ref, o_ref):
    a = _silu(c_ref[...]).astype(BF16)
    o_ref[...] = _dot(a, w_ref[...].astype(BF16)) + b_ref[...]


def _mod_table(cvecs, w_mod, b_mod):
    depth, d, n = w_mod.shape
    tn = 1024
    return pl.pallas_call(
        _mod_kernel,
        out_shape=jax.ShapeDtypeStruct((depth, SUBLANES, n), F32),
        grid=(depth, n // tn),
        in_specs=[
            pl.BlockSpec((SUBLANES, d), lambda l, j: (0, 0)),
            pl.BlockSpec((None, d, tn), lambda l, j: (l, 0, j)),
            pl.BlockSpec((None, 1, tn), lambda l, j: (l, 0, j)),
        ],
        out_specs=pl.BlockSpec((None, SUBLANES, tn), lambda l, j: (l, 0, j)),
        compiler_params=_cparams(("arbitrary", "arbitrary"), 40),
        name="mod_table",
    )(cvecs, w_mod, b_mod.reshape(depth, 1, n))


def _mod_row_index(chunk, chunks_per_seq, ctx_chunks, batch):
    within = chunk % chunks_per_seq
    return jnp.where(within < ctx_chunks, batch, chunk // chunks_per_seq)


def _rms(x, w):
    return x * lax.rsqrt(jnp.mean(x * x, axis=-1, keepdims=True) + EPS) * w


def _inproj_kernel(x_ref, mod_ref, nw_ref, w_ref, y_ref, h_ref, *, tm, d, chunks_per_seq, ctx_chunks, batch):
    i, j = pl.program_id(0), pl.program_id(1)

    @pl.when(j == 0)
    def _():
        for s in range(tm // MOD_CHUNK):
            row = _mod_row_index(i * (tm // MOD_CHUNK) + s, chunks_per_seq, ctx_chunks, batch)
            shift = mod_ref[pl.ds(row, 1), 0:d]
            scale = mod_ref[pl.ds(row, 1), d:2 * d]
            xs = x_ref[s * MOD_CHUNK:(s + 1) * MOD_CHUNK, :]
            h = _rms(xs, nw_ref[...]) * (1.0 + scale) + shift
            h_ref[s * MOD_CHUNK:(s + 1) * MOD_CHUNK, :] = h.astype(BF16)

    y_ref[...] = _dot(h_ref[...], w_ref[...])


def _inproj(x2d, mod_l, norm_w, w_in_bf, layer, *, chunks_per_seq, ctx_chunks, batch):
    r, d = x2d.shape
    tm, tn = 1024, 1024
    assert r % tm == 0 and N_PROJ % tn == 0
    kern = functools.partial(_inproj_kernel, tm=tm, d=d, chunks_per_seq=chunks_per_seq,
                             ctx_chunks=ctx_chunks, batch=batch)
    return pl.pallas_call(
        kern,
        out_shape=(jax.ShapeDtypeStruct((r, N_PROJ), F32), jax.ShapeDtypeStruct((r, d), BF16)),
        grid=(r // tm, N_PROJ // tn),
        in_specs=[
            pl.BlockSpec((tm, d), lambda i, j: (i, 0)),
            pl.BlockSpec(mod_l.shape, lambda i, j: (0, 0)),
            pl.BlockSpec((None, 1, d), lambda i, j: (layer, 0, 0)),
            pl.BlockSpec((None, d, tn), lambda i, j: (layer, 0, j)),
        ],
        out_specs=(pl.BlockSpec((tm, tn), lambda i, j: (i, j)),
                   pl.BlockSpec((tm, d), lambda i, j: (i, 0))),
        compiler_params=_cparams(("arbitrary", "arbitrary"), 52),
        name="inproj",
    )(x2d, mod_l, norm_w, w_in_bf)


def _hg_chunk(q, k, v, cum, st, keep, reverse):
    c = q.shape[0]
    tot = cum[0:1] if reverse else cum[c - 1:c]
    qe = (q * jnp.exp(cum)).astype(BF16)
    o_inter = _dot_nt(qe, st.astype(BF16))
    kd = k * jnp.exp(tot - cum)
    st_new = st * jnp.exp(tot) + _dot_tn(v.astype(BF16), kd.astype(BF16))

    rows = []
    for j in range(c // HG_BLK):
        lo = j * HG_BLK
        mid = lo + HG_BLK // 2 - (0 if reverse else 1)
        r = cum[mid:mid + 1]
        qt = (q[lo:lo + HG_BLK] * jnp.exp(cum[lo:lo + HG_BLK] - r)).astype(BF16)
        kh = (k * jnp.exp(jnp.minimum(r - cum, HG_EXP_CAP))).astype(BF16)
        rows.append(_dot_nt(qt, kh))
    a = jnp.where(keep, jnp.concatenate(rows, axis=0), 0.0)
    return _dot(a.astype(BF16), v.astype(BF16)) + o_inter, st_new


def _hgrn2_kernel(q_ref, ff_ref, fb_ref, i_ref, g_ref, lb_ref, nw_ref, o_ref, accf_ref, accb_ref, *, ctx, lt):
    nchunks = lt // HG_CHUNK
    nctx = ctx // HG_CHUNK
    dirs = []
    for direction, (f_ref, acc_ref) in enumerate(((ff_ref, accf_ref), (fb_ref, accb_ref))):
        keep, tri = _tri_matrix(HG_CHUNK, direction == 1)
        dirs.append((direction == 1, f_ref, acc_ref, keep, tri, lb_ref[direction:direction + 1, :]))

    def body(step, states):
        new_states = []
        for (reverse, f_ref, acc_ref, keep, tri, lbd), st in zip(dirs, states):
            ci = _scan_chunk(step, nchunks, nctx, reverse)
            rows = pl.ds(pl.multiple_of(ci * HG_CHUNK, HG_CHUNK), HG_CHUNK)
            q = _silu(q_ref[rows, :]) * (HG_DK ** -0.5)
            f = lbd + (1.0 - lbd) * _sigmoid(f_ref[rows, :])
            logf = jnp.log(jnp.maximum(f, MIN_FORGET))
            k = 1.0 - f
            v = i_ref[rows, :]
            cum = _tri_cumsum(tri, logf)
            o, st_new = _hg_chunk(q, k, v, cum, st, keep, reverse)
            acc_ref[rows, :] = o
            new_states.append(st_new)
        return tuple(new_states)

    zero = jnp.zeros((HG_DK, HG_DK), F32)
    lax.fori_loop(0, nchunks, body, (zero, zero), unroll=2)

    y = accf_ref[...] + accb_ref[...]
    o_ref[...] = (_rms(y, nw_ref[...]) * _silu(g_ref[...])).astype(o_ref.dtype)


def _hgrn2(y3, lb, norm_w, layer, ctx):
    b, lt, _ = y3.shape
    blk = lambda off: pl.BlockSpec((None, lt, HG_DK), lambda bi, h, off=off: (bi, 0, off + h))
    return pl.pallas_call(
        functools.partial(_hgrn2_kernel, ctx=ctx, lt=lt),
        out_shape=jax.ShapeDtypeStruct((b, lt, HG_WIDTH), BF16),
        grid=(b, HG_HEADS),
        in_specs=[blk(0), blk(HG_HEADS), blk(2 * HG_HEADS), blk(3 * HG_HEADS), blk(4 * HG_HEADS),
                  pl.BlockSpec((2, HG_DK), lambda bi, h: (0, h)),
                  pl.BlockSpec((None, 1, HG_DK), lambda bi, h: (layer, 0, 0))],
        out_specs=pl.BlockSpec((None, lt, HG_DK), lambda bi, h: (bi, 0, h)),
        scratch_shapes=[pltpu.VMEM((lt, HG_DK), F32)] * 2,
        compiler_params=_cparams(("arbitrary", "arbitrary"), 40),
        name="hgrn2",
    )(y3, y3, y3, y3, y3, lb, norm_w)


def _na_kernel(q_ref, k_ref, v_ref, bias_ref, qw_ref, kw_ref, o_ref, qs_ref, ks_ref, vs_ref, *, ctx, seq):
    rows = seq // GRID_W
    wr = min(NA_WIN_ROWS, rows)
    scale = NA_HD ** -0.5
    qs_ref[...] = (_rms(q_ref[...], qw_ref[...]) * scale).astype(BF16)
    ks_ref[...] = _rms(k_ref[...], kw_ref[...]).astype(BF16)
    vs_ref[...] = v_ref[...].astype(BF16)
    kc = ks_ref[0:ctx, :]
    vc = vs_ref[0:ctx, :]

    s = _dot_nt(qs_ref[0:ctx, :], kc)
    p = jnp.exp(s - jnp.max(s, axis=-1, keepdims=True))
    o = _dot(p.astype(BF16), vc) / jnp.sum(p, axis=-1, keepdims=True)
    o_ref[0:ctx, :] = o.astype(o_ref.dtype)

    def body(r, carry):
        r0 = jnp.clip(r - wr // 2, 0, rows - wr)
        qrow = pl.ds(pl.multiple_of(ctx + r * GRID_W, GRID_W), GRID_W)
        krow = pl.ds(pl.multiple_of(ctx + r0 * GRID_W, GRID_W), wr * GRID_W)
        qr = qs_ref[qrow, :]
        s_lat = _dot_nt(qr, ks_ref[krow, :]) + bias_ref[r0 - r + NA_WIN_ROWS - 1]
        s_ctx = _dot_nt(qr, kc)
        m = jnp.maximum(jnp.max(s_lat, axis=-1, keepdims=True), jnp.max(s_ctx, axis=-1, keepdims=True))
        p_lat = jnp.exp(s_lat - m)
        p_ctx = jnp.exp(s_ctx - m)
        den = jnp.sum(p_lat, axis=-1, keepdims=True) + jnp.sum(p_ctx, axis=-1, keepdims=True)
        o = (_dot(p_lat.astype(BF16), vs_ref[krow, :]) + _dot(p_ctx.astype(BF16), vc)) / den
        o_ref[qrow, :] = o.astype(o_ref.dtype)
        return carry

    lax.fori_loop(0, rows, body, 0, unroll=2)


def _na_bias_table(rpb, rows):
    wr = min(NA_WIN_ROWS, rows)
    j = np.arange(GRID_W)
    c0 = np.clip(j - NA_WIN_COLS // 2, 0, GRID_W - NA_WIN_COLS)
    col_mask = (j[None, :] >= c0[:, None]) & (j[None, :] < c0[:, None] + NA_WIN_COLS)
    dc = np.clip(j[None, :] - j[:, None], -(NA_WIN_COLS - 1), NA_WIN_COLS - 1) + NA_WIN_COLS - 1
    onehot = jnp.asarray(dc[:, :, None] == np.arange(2 * NA_WIN_COLS - 1)[None, None, :], F32)
    base = jnp.einsum("hrd,qcd->hrqc", rpb.astype(F32), onehot, precision=lax.Precision.HIGHEST)
    base = jnp.where(col_mask[None, None], base, NEG_BIG)
    t = jnp.stack([base[:, i:i + wr] for i in range(NA_WIN_ROWS)], axis=1)
    return t.transpose(0, 1, 3, 2, 4).reshape(rpb.shape[0], NA_WIN_ROWS, GRID_W, wr * GRID_W)


def _nattn(y3, bias_tab, qw, kw, layer, ctx):
    b, lt, _ = y3.shape
    seq = lt - ctx
    base = C_NA // NA_HD
    blk = lambda off: pl.BlockSpec((None, lt, NA_HD), lambda bi, h, off=off: (bi, 0, base + off + h))
    return pl.pallas_call(
        functools.partial(_na_kernel, ctx=ctx, seq=seq),
        out_shape=jax.ShapeDtypeStruct((b, lt, NA_WIDTH), BF16),
        grid=(b, NA_HEADS),
        in_specs=[blk(0), blk(NA_HEADS), blk(2 * NA_HEADS),
                  pl.BlockSpec((None,) + bias_tab.shape[1:], lambda bi, h: (h, 0, 0, 0)),
                  pl.BlockSpec((None, 1, NA_HD), lambda bi, h: (layer, 0, 0)),
                  pl.BlockSpec((None, 1, NA_HD), lambda bi, h: (layer, 0, 0))],
        out_specs=pl.BlockSpec((None, lt, NA_HD), lambda bi, h: (bi, 0, h)),
        scratch_shapes=[pltpu.VMEM((lt, NA_HD), BF16)] * 3,
        compiler_params=_cparams(("arbitrary", "arbitrary"), 40),
        name="nattn",
    )(y3, y3, y3, bias_tab, qw, kw)


def _rope_tables(seq):
    half = SSD_STATE // 2
    pos = np.arange(seq)
    inv_freq = 1.0 / (ROPE_BASE ** (np.arange(0, half, 2, dtype=np.float64) / half))
    lane = np.arange(SSD_STATE)
    p = np.where(lane[None, :] < half, (pos // GRID_W)[:, None], (pos % GRID_W)[:, None])
    ang = p * inv_freq[lane % (half // 2)][None, :]
    sign = np.where((lane % half) < half // 2, -1.0, 1.0)
    return jnp.asarray(np.cos(ang), F32), jnp.asarray(np.sin(ang) * sign[None, :], F32)


def _conv_silu(x, w, bias, ctx, lt):
    t = lax.broadcasted_iota(jnp.int32, (lt, 1), 0)
    pos = jnp.where(t < ctx, t, t - ctx)
    length = jnp.where(t < ctx, ctx, lt - ctx)
    left = (SSD_CONV - 1) // 2
    acc = bias + jnp.zeros_like(x)
    for j in range(SSD_CONV):
        off = j - left
        xs = x if off == 0 else pltpu.roll(x, (-off) % lt, axis=0)
        ok = (pos + off >= 0) & (pos + off < length)
        acc = acc + jnp.where(ok, xs, 0.0) * w[j:j + 1, :]
    return _silu(acc)


def _rope(t, cos, sin_signed):
    lane = lax.broadcasted_iota(jnp.int32, t.shape, 1)
    quarter = SSD_STATE // 4
    swapped = jnp.where((lane & quarter) == 0, pltpu.roll(t, SSD_STATE - quarter, axis=1),
                        pltpu.roll(t, quarter, axis=1))
    return t * cos + swapped * sin_signed


def _ssd_kernel(x_ref, b0_ref, b1_ref, c0_ref, c1_ref, dt_ref, wx_ref, wb0_ref, wb1_ref, wc0_ref, wc1_ref,
                bx_ref, bb0_ref, bb1_ref, bc0_ref, bc1_ref, dtb_ref, alog_ref, dskip_ref, cos_ref, sin_ref,
                o_ref, xs_ref, bs0_ref, bs1_ref, cs0_ref, cs1_ref, dts_ref, yb_ref, cumc_ref, cumt_ref, dtt_ref,
                bt0_ref, bt1_ref, *, ctx, lt):
    nchunks = lt // SSD_CHUNK
    nctx = ctx // SSD_CHUNK
    pair = pl.program_id(1)

    xs_ref[...] = _conv_silu(x_ref[...], wx_ref[...], bx_ref[...], ctx, lt)
    for src, w, bb, dst in ((b0_ref, wb0_ref, bb0_ref, bs0_ref), (b1_ref, wb1_ref, bb1_ref, bs1_ref),
                            (c0_ref, wc0_ref, bc0_ref, cs0_ref), (c1_ref, wc1_ref, bc1_ref, cs1_ref)):
        t = _conv_silu(src[...], w[...], bb[...], ctx, lt)
        dst[0:ctx, :] = t[0:ctx].astype(BF16)
        dst[ctx:lt, :] = _rope(t[ctx:lt], cos_ref[...], sin_ref[...]).astype(BF16)
    raw = dt_ref[...] + dtb_ref[...]
    dts_ref[...] = jnp.maximum(raw, 0.0) + jnp.log(1.0 + jnp.exp(-jnp.abs(raw)))

    lane = lax.broadcasted_iota(jnp.int32, (SSD_CHUNK, LANES), 1)
    a_neg = -jnp.exp(alog_ref[...])
    masks = [_tri_matrix(SSD_CHUNK, reverse) for reverse in (False, True)]

    def prep(ci, carry):
        rows = pl.ds(pl.multiple_of(ci * SSD_CHUNK, SSD_CHUNK), SSD_CHUNK)
        dt_all = dts_ref[rows, :]
        dtt_ref[ci] = dt_all.T
        for direction in range(2):
            cum = _tri_cumsum(masks[direction][1], dt_all * a_neg)
            cumc_ref[direction, rows, :] = cum
            cumt_ref[direction, ci] = cum.T
        bt0_ref[ci] = bs0_ref[rows, :].astype(F32).T.astype(BF16)
        bt1_ref[ci] = bs1_ref[rows, :].astype(F32).T.astype(BF16)
        return carry

    lax.fori_loop(0, nchunks, prep, 0, unroll=2)

    def body(step, states):
        new_states = []
        for direction in range(2):
            reverse = direction == 1
            keep, tri = masks[direction]
            ci = _scan_chunk(step, nchunks, nctx, reverse)
            rows = pl.ds(pl.multiple_of(ci * SSD_CHUNK, SSD_CHUNK), SSD_CHUNK)
            x = xs_ref[rows, :]
            dt_all = dts_ref[rows, :]
            cum_all = cumc_ref[direction, rows, :]
            y = jnp.zeros((SSD_CHUNK, LANES), F32)
            for hh, (bt_ref, cs_ref) in enumerate(((bt0_ref, cs0_ref), (bt1_ref, cs1_ref))):
                chan = direction * SSD_HEADS + 2 * pair + hh
                sel = lane == chan
                dt_c = jnp.sum(jnp.where(sel, dt_all, 0.0), axis=-1, keepdims=True)
                cum_c = jnp.sum(jnp.where(sel, cum_all, 0.0), axis=-1, keepdims=True)
                dt_cb = jnp.broadcast_to(dt_c, (SSD_CHUNK, LANES))
                cum_cb = jnp.broadcast_to(cum_c, (SSD_CHUNK, LANES))
                cum_rb = cumt_ref[direction, ci, pl.ds(chan, 1), :]
                dt_rb = dtt_ref[ci, pl.ds(chan, 1), :]
                tot = cum_cb[0:1] if reverse else cum_cb[SSD_CHUNK - 1:SSD_CHUNK]
                seg = jnp.where(keep, jnp.exp(jnp.where(keep, cum_cb - cum_rb, 0.0)), 0.0)
                bt = bt_ref[ci]
                cm = cs_ref[rows, :]
                m = _dot(cm, bt) * seg * dt_rb
                half_mask = (lane // SSD_HD) == hh
                xm = jnp.where(half_mask, x, 0.0)
                st = states[2 * direction + hh]
                yh = _dot(m.astype(BF16), xm.astype(BF16)) + _dot(cm, st.astype(BF16)) * jnp.exp(cum_cb)
                w_end = dt_cb * jnp.exp(tot - cum_cb)
                st_new = st * jnp.exp(tot) + _dot(bt, (xm * w_end).astype(BF16))
                y = y + yh
                new_states.append(st_new)
            if direction == 0:
                o_ref[rows, :] = y + dskip_ref[...] * x
            else:
                yb_ref[rows, :] = y
        return tuple(new_states)

    zero = jnp.zeros((SSD_STATE, LANES), F32)
    lax.fori_loop(0, nchunks, body, (zero,) * 4, unroll=2)
    o_ref[...] += yb_ref[...]


def _ssd(y3, conv_w, conv_b, dt_bias_l, a_log_l, d_lanes, cos_t, sin_t, layer, ctx):
    b, lt, _ = y3.shape
    npair = SSD_HEADS // 2
    hg = SSD_HEADS // SSD_GROUPS
    xb = C_XBC // LANES
    bb = xb + SSD_WIDTH // LANES
    cb = bb + SSD_GROUPS
    g0 = lambda p: (2 * p) // hg
    g1 = lambda p: (2 * p + 1) // hg
    yblk = lambda f: pl.BlockSpec((None, lt, LANES), lambda bi, p, f=f: (bi, 0, f(p)))
    wblk = lambda f: pl.BlockSpec((None, SSD_CONV, LANES), lambda bi, p, f=f: (layer, 0, f(p)))
    bblk = lambda f: pl.BlockSpec((None, 1, LANES), lambda bi, p, f=f: (layer, 0, f(p)))
    cols = [lambda p: p, lambda p: npair + g0(p), lambda p: npair + g1(p),
            lambda p: npair + SSD_GROUPS + g0(p), lambda p: npair + SSD_GROUPS + g1(p)]
    const2 = lambda shape: pl.BlockSpec(shape, lambda bi, p: (0, 0))
    return pl.pallas_call(
        functools.partial(_ssd_kernel, ctx=ctx, lt=lt),
        out_shape=jax.ShapeDtypeStruct((b, lt, SSD_WIDTH), F32),
        grid=(b, npair),
        in_specs=[yblk(lambda p: xb + p), yblk(lambda p: bb + g0(p)), yblk(lambda p: bb + g1(p)),
                  yblk(lambda p: cb + g0(p)), yblk(lambda p: cb + g1(p)), yblk(lambda p: C_DT // LANES)]
                 + [wblk(f) for f in cols] + [bblk(f) for f in cols]
                 + [const2((1, LANES)), const2((1, LANES)),
                    pl.BlockSpec((None, 1, LANES), lambda bi, p: (layer, 0, p)),
                    const2(cos_t.shape), const2(sin_t.shape)],
        out_specs=pl.BlockSpec((None, lt, LANES), lambda bi, p: (bi, 0, p)),
        scratch_shapes=[pltpu.VMEM((lt, LANES), F32)] + [pltpu.VMEM((lt, LANES), BF16)] * 4
                       + [pltpu.VMEM((lt, LANES), F32)] * 2
                       + [pltpu.VMEM((2, lt, LANES), F32),
                          pltpu.VMEM((2, lt // SSD_CHUNK, SSD_CHUNK, LANES), F32),
                          pltpu.VMEM((lt // SSD_CHUNK, SSD_CHUNK, LANES), F32)]
                       + [pltpu.VMEM((lt // SSD_CHUNK, SSD_STATE, SSD_CHUNK), BF16)] * 2,
        compiler_params=_cparams(("arbitrary", "arbitrary"), 48),
        name="ssd",
    )(y3, y3, y3, y3, y3, y3, *([conv_w] * 5), *([conv_b] * 5), dt_bias_l, a_log_l, d_lanes, cos_t, sin_t)


def _mix_kernel(h_ref, yh_ref, yn_ref, ys_ref, z_ref, snw_ref, wh_ref, wn_ref, ws_ref, g0_ref, g1_ref, g2_ref,
                m_ref, ysn_ref):
    @pl.when(pl.program_id(1) == 0)
    def _():
        y = ys_ref[...] * _silu(z_ref[...])
        gw = SSD_WIDTH // SSD_GROUPS
        lane = lax.broadcasted_iota(jnp.int32, y.shape, 1)
        inv = jnp.zeros_like(y)
        for g in range(SSD_GROUPS):
            in_g = (lane >= g * gw) & (lane < (g + 1) * gw)
            ms = jnp.sum(jnp.where(in_g, y * y, 0.0), axis=-1, keepdims=True) * (1.0 / gw)
            inv = jnp.where(in_g, lax.rsqrt(ms + EPS), inv)
        ysn_ref[...] = (y * inv * snw_ref[...]).astype(BF16)

    h = h_ref[...]
    m = _sigmoid(_dot(h, g0_ref[...])) * _dot(yh_ref[...], wh_ref[...])
    m = m + _sigmoid(_dot(h, g1_ref[...])) * _dot(yn_ref[...], wn_ref[...])
    m = m + _sigmoid(_dot(h, g2_ref[...])) * _dot(ysn_ref[...], ws_ref[...])
    m_ref[...] = m.astype(m_ref.dtype)


def _mix(h, y_hg, y_na, y_ssd, y2d, ssd_nw, wb_hg, wb_na, wb_ssd, w_gate_bf, layer):
    r, d = h.shape
    tm, tn = 512, 512
    nt = d // tn
    row = lambda w: pl.BlockSpec((tm, w), lambda i, j: (i, 0))
    wspec = lambda k: pl.BlockSpec((None, k, tn), lambda i, j: (layer, 0, j))
    gspec = lambda br: pl.BlockSpec((None, d, tn), lambda i, j, br=br: (layer, 0, br * nt + j))
    return pl.pallas_call(
        _mix_kernel,
        out_shape=jax.ShapeDtypeStruct((r, d), BF16),
        grid=(r // tm, nt),
        in_specs=[row(d), row(HG_WIDTH), row(NA_WIDTH), row(SSD_WIDTH),
                  pl.BlockSpec((tm, SSD_WIDTH), lambda i, j: (i, C_Z // SSD_WIDTH)),
                  pl.BlockSpec((None, 1, SSD_WIDTH), lambda i, j: (layer, 0, 0)),
                  wspec(HG_WIDTH), wspec(NA_WIDTH), wspec(SSD_WIDTH), gspec(0), gspec(1), gspec(2)],
        out_specs=pl.BlockSpec((tm, tn), lambda i, j: (i, j)),
        scratch_shapes=[pltpu.VMEM((tm, SSD_WIDTH), BF16)],
        compiler_params=_cparams(("arbitrary", "arbitrary"), 48),
        name="mix",
    )(h, y_hg, y_na, y_ssd, y2d, ssd_nw, wb_hg, wb_na, wb_ssd, w_gate_bf, w_gate_bf, w_gate_bf)


def _outproj_kernel(x_ref, m_ref, mod_ref, nw_ref, wo_ref, wr_ref, br_ref, xo_ref, h2_ref, lg_ref,
                    *, d, chunks_per_seq, ctx_chunks, batch):
    row = _mod_row_index(pl.program_id(0), chunks_per_seq, ctx_chunks, batch)
    g1 = mod_ref[pl.ds(row, 1), 2 * d:3 * d]
    sh2 = mod_ref[pl.ds(row, 1), 3 * d:4 * d]
    sc2 = mod_ref[pl.ds(row, 1), 4 * d:5 * d]
    x = x_ref[...] + g1 * _dot(m_ref[...], wo_ref[...])
    xo_ref[...] = x
    h2 = _rms(x, nw_ref[...]) * (1.0 + sc2) + sh2
    h2_ref[...] = h2
    a, b, _ = _split3(h2)
    wa, wb, _ = _split3(wr_ref[...])
    lg_ref[...] = _dot(a, wa) + _dot(a, wb) + _dot(b, wa) + br_ref[...]


def _outproj(x2d, m, mod_l, norm_w, w_out_bf, w_router, b_router, layer, *, chunks_per_seq, ctx_chunks, batch):
    r, d = x2d.shape
    tm = MOD_CHUNK
    kern = functools.partial(_outproj_kernel, d=d, chunks_per_seq=chunks_per_seq, ctx_chunks=ctx_chunks,
                             batch=batch)
    return pl.pallas_call(
        kern,
        out_shape=(jax.ShapeDtypeStruct((r, d), F32), jax.ShapeDtypeStruct((r, d), F32),
                   jax.ShapeDtypeStruct((r, LANES), F32)),
        grid=(r // tm,),
        in_specs=[pl.BlockSpec((tm, d), lambda i: (i, 0)),
                  pl.BlockSpec((tm, d), lambda i: (i, 0)),
                  pl.BlockSpec(mod_l.shape, lambda i: (0, 0)),
                  pl.BlockSpec((None, 1, d), lambda i: (layer, 0, 0)),
                  pl.BlockSpec((None, d, d), lambda i: (layer, 0, 0)),
                  pl.BlockSpec((None, d, LANES), lambda i: (layer, 0, 0)),
                  pl.BlockSpec((None, 1, LANES), lambda i: (layer, 0, 0))],
        out_specs=(pl.BlockSpec((tm, d), lambda i: (i, 0)), pl.BlockSpec((tm, d), lambda i: (i, 0)),
                   pl.BlockSpec((tm, LANES), lambda i: (i, 0))),
        compiler_params=_cparams(("arbitrary",), 48),
        name="outproj",
    )(x2d, m, mod_l, norm_w, w_out_bf, w_router, b_router)


def _moe_kernel(ie_ref, ib_ref, in_ref, if_ref, x_hbm, wg_ref, wu_ref, wd_ref, bg_ref, bu_ref, bd_ref, y_hbm,
                xbuf, acc, sem_in, sem_out, *, nj, n_items):
    it, j = pl.program_id(0), pl.program_id(1)
    nblk = in_ref[it]
    nfill = if_ref[it]
    row0 = pl.multiple_of(ib_ref[it] * MOE_RB, MOE_RB)

    def in_copy(c):
        return pltpu.make_async_copy(x_hbm.at[pl.ds(row0 + c * MOE_RB, MOE_RB), :],
                                     acc.at[pl.ds(c * MOE_RB, MOE_RB), :], sem_in)

    def out_copy(c, src_block):
        return pltpu.make_async_copy(acc.at[pl.ds(src_block * MOE_RB, MOE_RB), :],
                                     y_hbm.at[pl.ds(row0 + c * MOE_RB, MOE_RB), :], sem_out)

    def wait_out(n):
        one = pltpu.make_async_copy(acc.at[pl.ds(0, MOE_RB), :], y_hbm.at[pl.ds(0, MOE_RB), :], sem_out)
        lax.fori_loop(0, n, lambda c, _: (one.wait(), 0)[1], 0)

    n_prev = jnp.where(it > 0, in_ref[jnp.maximum(it - 1, 0)], 0)

    @pl.when((it == 0) & (j == 0))
    def _():
        acc[...] = jnp.zeros(acc.shape, F32)

    @pl.when(j == 0)
    def _():
        wait_out(n_prev)

    @pl.when((nfill > 0) & (j == 0))
    def _():
        acc[0:MOE_RB, :] = jnp.zeros((MOE_RB, acc.shape[1]), F32)
        lax.fori_loop(0, nfill, lambda c, _: (out_copy(c, 0).start(), 0)[1], 0)
        wait_out(nfill)

    @pl.when(nblk > 0)
    def _():
        @pl.when(j == 0)
        def _():
            lax.fori_loop(0, nblk, lambda c, _: (in_copy(c).start(), 0)[1], 0)
            lax.fori_loop(0, nblk, lambda c, _: (in_copy(c).wait(), 0)[1], 0)
            xbuf[...] = acc[...].astype(BF16)

        x = xbuf[...]
        gate = jnp.minimum(_dot(x, wg_ref[...].astype(BF16)) + bg_ref[...], SWIGLU_LIMIT)
        up = jnp.clip(_dot(x, wu_ref[...].astype(BF16)) + bu_ref[...], -SWIGLU_LIMIT, SWIGLU_LIMIT)
        a = (gate * _sigmoid(SWIGLU_ALPHA * gate) * (up + 1.0)).astype(BF16)
        d = acc.shape[1]
        for n in range(d // MOE_TF):
            cols = slice(n * MOE_TF, (n + 1) * MOE_TF)
            part = _dot(a, wd_ref[:, cols].astype(BF16))
            base = jnp.where(j == 0, jnp.broadcast_to(bd_ref[:, cols], part.shape), acc[:, cols])
            acc[:, cols] = base + part

        @pl.when(j == nj - 1)
        def _():
            lax.fori_loop(0, nblk, lambda c, _: (out_copy(c, c).start(), 0)[1], 0)

            @pl.when(it == n_items - 1)
            def _():
                wait_out(nblk)


def _moe_experts(item_e, item_blk, item_n, item_fill, x_sorted, w_gate, b_gate, w_up, b_up, w_down, b_down, layer):
    cap, d = x_sorted.shape
    f = w_gate.shape[-1]
    nj = f // MOE_TF
    n_items = item_e.shape[0]
    depth, ne = b_gate.shape[:2]

    def jsel(it, j, n):
        return jnp.where(n[it] > 0, j, nj - 1)

    w_in = lambda: pl.BlockSpec((None, None, d, MOE_TF), lambda it, j, e, b, n, fl: (layer, e[it], 0, jsel(it, j, n)))
    b_in = lambda: pl.BlockSpec((None, None, 1, MOE_TF), lambda it, j, e, b, n, fl: (layer, e[it], 0, jsel(it, j, n)))
    grid_spec = pltpu.PrefetchScalarGridSpec(
        num_scalar_prefetch=4,
        grid=(n_items, nj),
        in_specs=[pl.BlockSpec(memory_space=pl.ANY), w_in(), w_in(),
                  pl.BlockSpec((None, None, MOE_TF, d), lambda it, j, e, b, n, fl: (layer, e[it], jsel(it, j, n), 0)),
                  b_in(), b_in(),
                  pl.BlockSpec((None, None, 1, d), lambda it, j, e, b, n, fl: (layer, e[it], 0, 0))],
        out_specs=pl.BlockSpec(memory_space=pl.ANY),
        scratch_shapes=[pltpu.VMEM((MOE_KB * MOE_RB, d), BF16), pltpu.VMEM((MOE_KB * MOE_RB, d), F32),
                        pltpu.SemaphoreType.DMA, pltpu.SemaphoreType.DMA],
    )
    return pl.pallas_call(
        functools.partial(_moe_kernel, nj=nj, n_items=n_items),
        out_shape=jax.ShapeDtypeStruct((cap, d), F32),
        grid_spec=grid_spec,
        compiler_params=_cparams(("arbitrary", "arbitrary"), 60),
        name="moe_experts",
    )(item_e, item_blk, item_n, item_fill, x_sorted, w_gate, w_up, w_down,
      b_gate.reshape(depth, ne, 1, f), b_up.reshape(depth, ne, 1, f), b_down.reshape(depth, ne, 1, d))


def _latent_chunk(i, chunks_per_seq, ctx_chunks):
    lat = chunks_per_seq - ctx_chunks
    return (i // lat) * chunks_per_seq + ctx_chunks + i % lat


def _combine_kernel(x_ref, y_ref, w_ref, mod_ref, o_ref, *, d, chunks_per_seq, ctx_chunks, batch, latent_only):
    chunk = pl.program_id(0)
    if latent_only:
        chunk = _latent_chunk(chunk, chunks_per_seq, ctx_chunks)
    row = _mod_row_index(chunk, chunks_per_seq, ctx_chunks, batch)
    g2 = mod_ref[pl.ds(row, 1), 5 * d:6 * d]
    w = w_ref[...]
    f = y_ref[0] * w[:, 0:1]
    for k in range(1, TOP_K):
        f = f + y_ref[k] * w[:, k:k + 1]
    o_ref[...] = x_ref[...] + g2 * f


def _combine(x2d, yk, top_w, mod_l, latent_only, *, chunks_per_seq, ctx_chunks, batch):
    d = x2d.shape[1]
    r = yk.shape[1]
    tm = MOD_CHUNK
    kern = functools.partial(_combine_kernel, d=d, chunks_per_seq=chunks_per_seq, ctx_chunks=ctx_chunks,
                             batch=batch, latent_only=latent_only)
    x_chunk = (lambda i: (_latent_chunk(i, chunks_per_seq, ctx_chunks), 0)) if latent_only else (lambda i: (i, 0))
    return pl.pallas_call(
        kern,
        out_shape=jax.ShapeDtypeStruct((r, d), F32),
        grid=(r // tm,),
        in_specs=[pl.BlockSpec((tm, d), x_chunk),
                  pl.BlockSpec((TOP_K, tm, d), lambda i: (0, i, 0)),
                  pl.BlockSpec((tm, TOP_K), lambda i: (i, 0)),
                  pl.BlockSpec(mod_l.shape, lambda i: (0, 0))],
        out_specs=pl.BlockSpec((tm, d), lambda i: (i, 0)),
        compiler_params=_cparams(("arbitrary",), 40),
        name="combine",
    )(x2d, yk, top_w, mod_l)


def _moe_plan(logits, n_tok):
    top_val, top_idx = lax.top_k(logits, TOP_K)
    top_w = jax.nn.softmax(top_val, axis=-1)
    n_assign = n_tok * TOP_K
    exp_id = top_idx.reshape(-1)
    onehot = (exp_id[:, None] == jnp.arange(N_EXPERTS)[None, :]).astype(jnp.int32)
    csum = jnp.cumsum(onehot, axis=0)
    rank = jnp.sum((csum - onehot) * onehot, axis=1)
    counts = csum[-1]
    nblk = (counts + MOE_RB - 1) // MOE_RB
    blk_end = jnp.cumsum(nblk)
    blk_start = blk_end - nblk
    dest = blk_start[exp_id] * MOE_RB + rank
    total_blocks = (n_assign + N_EXPERTS * (MOE_RB - 1)) // MOE_RB
    cap = total_blocks * MOE_RB
    slot_tok = jnp.zeros((cap,), jnp.int32).at[dest].set(jnp.arange(n_assign, dtype=jnp.int32) // TOP_K)

    n_items = N_EXPERTS + total_blocks // MOE_KB
    items_e = (nblk + MOE_KB - 1) // MOE_KB
    item_end = jnp.cumsum(items_e)
    item_start = item_end - items_e
    idx = jnp.arange(n_items)
    e_of = jnp.minimum(jnp.searchsorted(item_end, idx, side="right"), N_EXPERTS - 1).astype(jnp.int32)
    valid = idx < item_end[-1]
    e_last = e_of[jnp.maximum(item_end[-1] - 1, 0)]
    kk = idx - item_start[e_of]
    item_e = jnp.where(valid, e_of, e_last).astype(jnp.int32)
    fill_blk = blk_end[-1] + (idx - item_end[-1]) * MOE_KB
    item_blk = jnp.where(valid, blk_start[e_of] + kk * MOE_KB, jnp.minimum(fill_blk, total_blocks)).astype(jnp.int32)
    item_n = jnp.where(valid, jnp.clip(nblk[e_of] - kk * MOE_KB, 0, MOE_KB), 0).astype(jnp.int32)
    item_fill = jnp.where(valid, 0, jnp.clip(total_blocks - fill_blk, 0, MOE_KB)).astype(jnp.int32)
    return top_w, dest.reshape(n_tok, TOP_K), slot_tok, item_e, item_blk, item_n, item_fill


def kernel(x, c, ctx, c_ctx, w_mod, b_mod, norm1_w, norm2_w, w_in, hg_lb_logits, hg_norm_w, na_q_norm_w,
           na_k_norm_w, na_rpb, ssd_conv_w, ssd_conv_b, ssd_dt_bias, ssd_a_log, ssd_d, ssd_norm_w, w_branch_hg,
           w_branch_na, w_branch_ssd, w_out, moe_w_router, moe_b_router, moe_w_gate, moe_b_gate, moe_w_up,
           moe_b_up, moe_w_down, moe_b_down):
    batch, seq, d = x.shape
    n_ctx = ctx.shape[1]
    depth = w_mod.shape[0]
    lt = n_ctx + seq
    r = batch * lt
    chunks = dict(chunks_per_seq=lt // MOD_CHUNK, ctx_chunks=n_ctx // MOD_CHUNK, batch=batch)
    assert n_ctx % MOD_CHUNK == 0 and seq % MOD_CHUNK == 0 and batch < SUBLANES

    lb_soft = jax.nn.softmax(hg_lb_logits.astype(F32), axis=0)
    lower_bounds = jnp.cumsum(lb_soft, axis=0) - lb_soft[0]
    w_proj_bf = w_in[:, :, :N_PROJ].astype(BF16)
    w_gate_bf = w_in[:, :, N_MIX_COLS:].astype(BF16)
    wb_hg, wb_na, wb_ssd = (w.astype(BF16) for w in (w_branch_hg, w_branch_na, w_branch_ssd))
    w_out_bf = w_out.astype(BF16)
    w_router = jnp.pad(moe_w_router, ((0, 0), (0, 0), (0, LANES - N_EXPERTS)))
    b_router = jnp.pad(moe_b_router, ((0, 0), (0, LANES - N_EXPERTS))).reshape(depth, 1, LANES)
    vec = lambda p: p.reshape(depth, 1, p.shape[-1])
    dt_pad = lambda p: jnp.pad(p.reshape(depth, 2 * SSD_HEADS), ((0, 0), (0, LANES - 2 * SSD_HEADS)))
    dt_bias_p, a_log_p = dt_pad(ssd_dt_bias), dt_pad(ssd_a_log)
    d_lanes = jnp.repeat(ssd_d, SSD_HD, axis=-1).reshape(depth, 1, SSD_WIDTH)
    cos_t, sin_t = _rope_tables(seq)
    conv_b = vec(ssd_conv_b)

    cvecs = jnp.zeros((SUBLANES, d), F32).at[:batch].set(c).at[batch].set(c_ctx)
    mod = _mod_table(cvecs, w_mod, b_mod)

    xa = jnp.concatenate([ctx, x], axis=1).reshape(r, d)
    for layer in range(depth):
        mod_l = mod[layer]
        y2d, h = _inproj(xa, mod_l, vec(norm1_w), w_proj_bf, layer, **chunks)
        y3 = y2d.reshape(batch, lt, N_PROJ)
        y_hg = _hgrn2(y3, lower_bounds[layer], vec(hg_norm_w), layer, n_ctx)
        bias_tab = _na_bias_table(na_rpb[layer], seq // GRID_W)
        y_na = _nattn(y3, bias_tab, vec(na_q_norm_w), vec(na_k_norm_w), layer, n_ctx)
        y_ssd = _ssd(y3, ssd_conv_w, conv_b, dt_bias_p[layer:layer + 1], a_log_p[layer:layer + 1], d_lanes,
                     cos_t, sin_t, layer, n_ctx)
        m = _mix(h, y_hg.reshape(r, -1), y_na.reshape(r, -1), y_ssd.reshape(r, -1), y2d, vec(ssd_norm_w),
                 wb_hg, wb_na, wb_ssd, w_gate_bf, layer)
        xa, h2, logits = _outproj(xa, m, mod_l, vec(norm2_w), w_out_bf, w_router, b_router, layer, **chunks)

        latent_only = layer == depth - 1
        if latent_only:
            logits = logits.reshape(batch, lt, LANES)[:, n_ctx:].reshape(batch * seq, LANES)
        n_tok = logits.shape[0]
        top_w, dest, slot_tok, item_e, item_blk, item_n, item_fill = _moe_plan(logits[:, :N_EXPERTS], n_tok)
        slot_row = (slot_tok // seq) * lt + n_ctx + slot_tok % seq if latent_only else slot_tok
        x_sorted = jnp.take(h2, slot_row, axis=0, mode="clip")
        y_sorted = _moe_experts(item_e, item_blk, item_n, item_fill, x_sorted, moe_w_gate, moe_b_gate, moe_w_up, moe_b_up,
                                moe_w_down, moe_b_down, layer)
        yk = jnp.take(y_sorted, dest.T, axis=0, mode="clip")
        xa = _combine(xa, yk, top_w, mod_l, latent_only, **chunks)
    return xa.reshape(batch, seq, d)
```

```python
import functools
import math

import numpy as np
import jax
import jax.numpy as jnp
from jax import lax
from jax.experimental import pallas as pl
from jax.experimental.pallas import tpu as pltpu

F32 = jnp.float32
BF16 = jnp.bfloat16

LANES = 128
SUBLANES = 8
VMEM_BYTES_V7X = 64 * 1024 * 1024

GRID_W = 64
EPS = 1e-6
NEG_BIG = -1e30
MIN_FORGET = 1e-6
HG_HEADS, HG_DK = 6, 128
HG_WIDTH = HG_HEADS * HG_DK
HG_CHUNK = 64
HG_BLK = 8
HG_EXP_CAP = 60.0
NA_HEADS, NA_HD = 4, 128
NA_WIDTH = NA_HEADS * NA_HD
NA_WIN_ROWS, NA_WIN_COLS = 8, 16
SSD_HEADS, SSD_HD = 12, 64
SSD_WIDTH = SSD_HEADS * SSD_HD
SSD_GROUPS, SSD_STATE, SSD_CONV = 4, 128, 4
SSD_CHUNK = 128
SSD_CONV_CH = SSD_WIDTH + 2 * SSD_GROUPS * SSD_STATE
ROPE_BASE = 10000.0
N_EXPERTS, TOP_K = 32, 4
SWIGLU_ALPHA, SWIGLU_LIMIT = 1.702, 7.0

C_HG = 0
C_NA = 5 * HG_WIDTH
C_Z = C_NA + 3 * NA_WIDTH
C_XBC = C_Z + SSD_WIDTH
C_DT = C_XBC + SSD_CONV_CH
N_MIX_COLS = C_DT + 2 * SSD_HEADS
N_PROJ = 8192

MOD_CHUNK = 256
MOE_RB = 256
MOE_KB = 5
MOE_TF = 512


def _cparams(sem, vmem_mb):
    return pltpu.CompilerParams(dimension_semantics=sem, vmem_limit_bytes=vmem_mb * 1024 * 1024)


def _dot(a, b):
    return jnp.dot(a, b, preferred_element_type=F32)


def _dot_nt(a, b):
    return lax.dot_general(a, b, (((1,), (1,)), ((), ())), preferred_element_type=F32)


def _dot_tn(a, b):
    return lax.dot_general(a, b, (((0,), (0,)), ((), ())), preferred_element_type=F32)


def _sigmoid(x):
    return 1.0 / (1.0 + jnp.exp(-x))


def _silu(x):
    return x * _sigmoid(x)


def _split3(x):
    a = x.astype(BF16)
    r = x - a.astype(F32)
    b = r.astype(BF16)
    c = (r - b.astype(F32)).astype(BF16)
    return a, b, c


def _tri_cumsum(tri, g):
    a, b, c = _split3(g)
    return _dot(tri, a) + _dot(tri, b) + _dot(tri, c)


def _scan_chunk(step, nchunks, nctx, reverse):
    if not reverse:
        return step
    return jnp.where(step < nctx, nctx - 1 - step, nchunks - 1 - (step - nctx))


def _tri_matrix(n, reverse):
    t = lax.broadcasted_iota(jnp.int32, (n, n), 0)
    s = lax.broadcasted_iota(jnp.int32, (n, n), 1)
    keep = (s >= t) if reverse else (s <= t)
    return keep, jnp.where(keep, 1.0, 0.0).astype(BF16)


def _mod_kernel(c_ref, w_ref, b_ref, o_ref):
    a = _silu(c_ref[...]).astype(BF16)
    o_ref[...] = _dot(a, w_ref[...].astype(BF16)) + b_ref[...]


def _mod_table(cvecs, w_mod, b_mod):
    depth, d, n = w_mod.shape
    tn = 1024
    return pl.pallas_call(
        _mod_kernel,
        out_shape=jax.ShapeDtypeStruct((depth, SUBLANES, n), F32),
        grid=(depth, n // tn),
        in_specs=[
            pl.BlockSpec((SUBLANES, d), lambda l, j: (0, 0)),
            pl.BlockSpec((None, d, tn), lambda l, j: (l, 0, j)),
            pl.BlockSpec((None, 1, tn), lambda l, j: (l, 0, j)),
        ],
        out_specs=pl.BlockSpec((None, SUBLANES, tn), lambda l, j: (l, 0, j)),
        compiler_params=_cparams(("arbitrary", "arbitrary"), 40),
        name="mod_table",
    )(cvecs, w_mod, b_mod.reshape(depth, 1, n))


def _mod_row_index(chunk, chunks_per_seq, ctx_chunks, batch):
    within = chunk % chunks_per_seq
    return jnp.where(within < ctx_chunks, batch, chunk // chunks_per_seq)


def _rms(x, w):
    return x * lax.rsqrt(jnp.mean(x * x, axis=-1, keepdims=True) + EPS) * w


def _inproj_kernel(x_ref, mod_ref, nw_ref, w_ref, y_ref, h_ref, *, tm, d, chunks_per_seq, ctx_chunks, batch):
    i, j = pl.program_id(0), pl.program_id(1)

    @pl.when(j == 0)
    def _():
        for s in range(tm // MOD_CHUNK):
            row = _mod_row_index(i * (tm // MOD_CHUNK) + s, chunks_per_seq, ctx_chunks, batch)
            shift = mod_ref[pl.ds(row, 1), 0:d]
            scale = mod_ref[pl.ds(row, 1), d:2 * d]
            xs = x_ref[s * MOD_CHUNK:(s + 1) * MOD_CHUNK, :]
            h = _rms(xs, nw_ref[...]) * (1.0 + scale) + shift
            h_ref[s * MOD_CHUNK:(s + 1) * MOD_CHUNK, :] = h.astype(BF16)

    y_ref[...] = _dot(h_ref[...], w_ref[...])


def _inproj(x2d, mod_l, norm_w, w_in_bf, layer, *, chunks_per_seq, ctx_chunks, batch):
    r, d = x2d.shape
    tm, tn = 1024, 1024
    assert r % tm == 0 and N_PROJ % tn == 0
    kern = functools.partial(_inproj_kernel, tm=tm, d=d, chunks_per_seq=chunks_per_seq,
                             ctx_chunks=ctx_chunks, batch=batch)
    return pl.pallas_call(
        kern,
        out_shape=(jax.ShapeDtypeStruct((r, N_PROJ), F32), jax.ShapeDtypeStruct((r, d), BF16)),
        grid=(r // tm, N_PROJ // tn),
        in_specs=[
            pl.BlockSpec((tm, d), lambda i, j: (i, 0)),
            pl.BlockSpec(mod_l.shape, lambda i, j: (0, 0)),
            pl.BlockSpec((None, 1, d), lambda i, j: (layer, 0, 0)),
            pl.BlockSpec((None, d, tn), lambda i, j: (layer, 0, j)),
        ],
        out_specs=(pl.BlockSpec((tm, tn), lambda i, j: (i, j)),
                   pl.BlockSpec((tm, d), lambda i, j: (i, 0))),
        compiler_params=_cparams(("arbitrary", "arbitrary"), 52),
        name="inproj",
    )(x2d, mod_l, norm_w, w_in_bf)


def _hg_chunk(q, k, v, cum, st, keep, reverse):
    c = q.shape[0]
    tot = cum[0:1] if reverse else cum[c - 1:c]
    qe = (q * jnp.exp(cum)).astype(BF16)
    o_inter = _dot_nt(qe, st.astype(BF16))
    kd = k * jnp.exp(tot - cum)
    st_new = st * jnp.exp(tot) + _dot_tn(v.astype(BF16), kd.astype(BF16))

    rows = []
    for j in range(c // HG_BLK):
        lo = j * HG_BLK
        mid = lo + HG_BLK // 2 - (0 if reverse else 1)
        r = cum[mid:mid + 1]
        qt = (q[lo:lo + HG_BLK] * jnp.exp(cum[lo:lo + HG_BLK] - r)).astype(BF16)
        kh = (k * jnp.exp(jnp.minimum(r - cum, HG_EXP_CAP))).astype(BF16)
        rows.append(_dot_nt(qt, kh))
    a = jnp.where(keep, jnp.concatenate(rows, axis=0), 0.0)
    return _dot(a.astype(BF16), v.astype(BF16)) + o_inter, st_new


def _hgrn2_kernel(q_ref, ff_ref, fb_ref, i_ref, g_ref, lb_ref, nw_ref, o_ref, accf_ref, accb_ref, *, ctx, lt):
    nchunks = lt // HG_CHUNK
    nctx = ctx // HG_CHUNK
    dirs = []
    for direction, (f_ref, acc_ref) in enumerate(((ff_ref, accf_ref), (fb_ref, accb_ref))):
        keep, tri = _tri_matrix(HG_CHUNK, direction == 1)
        dirs.append((direction == 1, f_ref, acc_ref, keep, tri, lb_ref[direction:direction + 1, :]))

    def body(step, states):
        new_states = []
        for (reverse, f_ref, acc_ref, keep, tri, lbd), st in zip(dirs, states):
            ci = _scan_chunk(step, nchunks, nctx, reverse)
            rows = pl.ds(pl.multiple_of(ci * HG_CHUNK, HG_CHUNK), HG_CHUNK)
            q = _silu(q_ref[rows, :]) * (HG_DK ** -0.5)
            f = lbd + (1.0 - lbd) * _sigmoid(f_ref[rows, :])
            logf = jnp.log(jnp.maximum(f, MIN_FORGET))
            k = 1.0 - f
            v = i_ref[rows, :]
            cum = _tri_cumsum(tri, logf)
            o, st_new = _hg_chunk(q, k, v, cum, st, keep, reverse)
            acc_ref[rows, :] = o
            new_states.append(st_new)
        return tuple(new_states)

    zero = jnp.zeros((HG_DK, HG_DK), F32)
    lax.fori_loop(0, nchunks, body, (zero, zero), unroll=2)

    y = accf_ref[...] + accb_ref[...]
    o_ref[...] = (_rms(y, nw_ref[...]) * _silu(g_ref[...])).astype(o_ref.dtype)


def _hgrn2(y3, lb, norm_w, layer, ctx):
    b, lt, _ = y3.shape
    blk = lambda off: pl.BlockSpec((None, lt, HG_DK), lambda bi, h, off=off: (bi, 0, off + h))
    return pl.pallas_call(
        functools.partial(_hgrn2_kernel, ctx=ctx, lt=lt),
        out_shape=jax.ShapeDtypeStruct((b, lt, HG_WIDTH), BF16),
        grid=(b, HG_HEADS),
        in_specs=[blk(0), blk(HG_HEADS), blk(2 * HG_HEADS), blk(3 * HG_HEADS), blk(4 * HG_HEADS),
                  pl.BlockSpec((2, HG_DK), lambda bi, h: (0, h)),
                  pl.BlockSpec((None, 1, HG_DK), lambda bi, h: (layer, 0, 0))],
        out_specs=pl.BlockSpec((None, lt, HG_DK), lambda bi, h: (bi, 0, h)),
        scratch_shapes=[pltpu.VMEM((lt, HG_DK), F32)] * 2,
        compiler_params=_cparams(("arbitrary", "arbitrary"), 40),
        name="hgrn2",
    )(y3, y3, y3, y3, y3, lb, norm_w)


def _na_kernel(q_ref, k_ref, v_ref, bias_ref, qw_ref, kw_ref, o_ref, qs_ref, ks_ref, vs_ref, *, ctx, seq):
    rows = seq // GRID_W
    wr = min(NA_WIN_ROWS, rows)
    scale = NA_HD ** -0.5
    qs_ref[...] = (_rms(q_ref[...], qw_ref[...]) * scale).astype(BF16)
    ks_ref[...] = _rms(k_ref[...], kw_ref[...]).astype(BF16)
    vs_ref[...] = v_ref[...].astype(BF16)
    kc = ks_ref[0:ctx, :]
    vc = vs_ref[0:ctx, :]

    s = _dot_nt(qs_ref[0:ctx, :], kc)
    p = jnp.exp(s - jnp.max(s, axis=-1, keepdims=True))
    o = _dot(p.astype(BF16), vc) / jnp.sum(p, axis=-1, keepdims=True)
    o_ref[0:ctx, :] = o.astype(o_ref.dtype)

    def body(r, carry):
        r0 = jnp.clip(r - wr // 2, 0, rows - wr)
        qrow = pl.ds(pl.multiple_of(ctx + r * GRID_W, GRID_W), GRID_W)
        krow = pl.ds(pl.multiple_of(ctx + r0 * GRID_W, GRID_W), wr * GRID_W)
        qr = qs_ref[qrow, :]
        s_lat = _dot_nt(qr, ks_ref[krow, :]) + bias_ref[r0 - r + NA_WIN_ROWS - 1]
        s_ctx = _dot_nt(qr, kc)
        m = jnp.maximum(jnp.max(s_lat, axis=-1, keepdims=True), jnp.max(s_ctx, axis=-1, keepdims=True))
        p_lat = jnp.exp(s_lat - m)
        p_ctx = jnp.exp(s_ctx - m)
        den = jnp.sum(p_lat, axis=-1, keepdims=True) + jnp.sum(p_ctx, axis=-1, keepdims=True)
        o = (_dot(p_lat.astype(BF16), vs_ref[krow, :]) + _dot(p_ctx.astype(BF16), vc)) / den
        o_ref[qrow, :] = o.astype(o_ref.dtype)
        return carry

    lax.fori_loop(0, rows, body, 0, unroll=2)


def _na_bias_table(rpb, rows):
    wr = min(NA_WIN_ROWS, rows)
    j = np.arange(GRID_W)
    c0 = np.clip(j - NA_WIN_COLS // 2, 0, GRID_W - NA_WIN_COLS)
    col_mask = (j[None, :] >= c0[:, None]) & (j[None, :] < c0[:, None] + NA_WIN_COLS)
    dc = np.clip(j[None, :] - j[:, None], -(NA_WIN_COLS - 1), NA_WIN_COLS - 1) + NA_WIN_COLS - 1
    onehot = jnp.asarray(dc[:, :, None] == np.arange(2 * NA_WIN_COLS - 1)[None, None, :], F32)
    base = jnp.einsum("hrd,qcd->hrqc", rpb.astype(F32), onehot, precision=lax.Precision.HIGHEST)
    base = jnp.where(col_mask[None, None], base, NEG_BIG)
    t = jnp.stack([base[:, i:i + wr] for i in range(NA_WIN_ROWS)], axis=1)
    return t.transpose(0, 1, 3, 2, 4).reshape(rpb.shape[0], NA_WIN_ROWS, GRID_W, wr * GRID_W)


def _nattn(y3, bias_tab, qw, kw, layer, ctx):
    b, lt, _ = y3.shape
    seq = lt - ctx
    base = C_NA // NA_HD
    blk = lambda off: pl.BlockSpec((None, lt, NA_HD), lambda bi, h, off=off: (bi, 0, base + off + h))
    return pl.pallas_call(
        functools.partial(_na_kernel, ctx=ctx, seq=seq),
        out_shape=jax.ShapeDtypeStruct((b, lt, NA_WIDTH), BF16),
        grid=(b, NA_HEADS),
        in_specs=[blk(0), blk(NA_HEADS), blk(2 * NA_HEADS),
                  pl.BlockSpec((None,) + bias_tab.shape[1:], lambda bi, h: (h, 0, 0, 0)),
                  pl.BlockSpec((None, 1, NA_HD), lambda bi, h: (layer, 0, 0)),
                  pl.BlockSpec((None, 1, NA_HD), lambda bi, h: (layer, 0, 0))],
        out_specs=pl.BlockSpec((None, lt, NA_HD), lambda bi, h: (bi, 0, h)),
        scratch_shapes=[pltpu.VMEM((lt, NA_HD), BF16)] * 3,
        compiler_params=_cparams(("arbitrary", "arbitrary"), 40),
        name="nattn",
    )(y3, y3, y3, bias_tab, qw, kw)


def _rope_tables(seq):
    half = SSD_STATE // 2
    pos = np.arange(seq)
    inv_freq = 1.0 / (ROPE_BASE ** (np.arange(0, half, 2, dtype=np.float64) / half))
    lane = np.arange(SSD_STATE)
    p = np.where(lane[None, :] < half, (pos // GRID_W)[:, None], (pos % GRID_W)[:, None])
    ang = p * inv_freq[lane % (half // 2)][None, :]
    sign = np.where((lane % half) < half // 2, -1.0, 1.0)
    return jnp.asarray(np.cos(ang), F32), jnp.asarray(np.sin(ang) * sign[None, :], F32)


def _conv_silu(x, w, bias, ctx, lt):
    t = lax.broadcasted_iota(jnp.int32, (lt, 1), 0)
    pos = jnp.where(t < ctx, t, t - ctx)
    length = jnp.where(t < ctx, ctx, lt - ctx)
    left = (SSD_CONV - 1) // 2
    acc = bias + jnp.zeros_like(x)
    for j in range(SSD_CONV):
        off = j - left
        xs = x if off == 0 else pltpu.roll(x, (-off) % lt, axis=0)
        ok = (pos + off >= 0) & (pos + off < length)
        acc = acc + jnp.where(ok, xs, 0.0) * w[j:j + 1, :]
    return _silu(acc)


def _rope(t, cos, sin_signed):
    lane = lax.broadcasted_iota(jnp.int32, t.shape, 1)
    quarter = SSD_STATE // 4
    swapped = jnp.where((lane & quarter) == 0, pltpu.roll(t, SSD_STATE - quarter, axis=1),
                        pltpu.roll(t, quarter, axis=1))
    return t * cos + swapped * sin_signed


def _ssd_kernel(x_ref, b0_ref, b1_ref, c0_ref, c1_ref, dt_ref, wx_ref, wb0_ref, wb1_ref, wc0_ref, wc1_ref,
                bx_ref, bb0_ref, bb1_ref, bc0_ref, bc1_ref, dtb_ref, alog_ref, dskip_ref, cos_ref, sin_ref,
                o_ref, xs_ref, bs0_ref, bs1_ref, cs0_ref, cs1_ref, dts_ref, yb_ref, cumc_ref, cumt_ref, dtt_ref,
                bt0_ref, bt1_ref, *, ctx, lt):
    nchunks = lt // SSD_CHUNK
    nctx = ctx // SSD_CHUNK
    pair = pl.program_id(1)

    xs_ref[...] = _conv_silu(x_ref[...], wx_ref[...], bx_ref[...], ctx, lt)
    for src, w, bb, dst in ((b0_ref, wb0_ref, bb0_ref, bs0_ref), (b1_ref, wb1_ref, bb1_ref, bs1_ref),
                            (c0_ref, wc0_ref, bc0_ref, cs0_ref), (c1_ref, wc1_ref, bc1_ref, cs1_ref)):
        t = _conv_silu(src[...], w[...], bb[...], ctx, lt)
        dst[0:ctx, :] = t[0:ctx].astype(BF16)
        dst[ctx:lt, :] = _rope(t[ctx:lt], cos_ref[...], sin_ref[...]).astype(BF16)
    raw = dt_ref[...] + dtb_ref[...]
    dts_ref[...] = jnp.maximum(raw, 0.0) + jnp.log(1.0 + jnp.exp(-jnp.abs(raw)))

    lane = lax.broadcasted_iota(jnp.int32, (SSD_CHUNK, LANES), 1)
    a_neg = -jnp.exp(alog_ref[...])
    masks = [_tri_matrix(SSD_CHUNK, reverse) for reverse in (False, True)]

    def prep(ci, carry):
        rows = pl.ds(pl.multiple_of(ci * SSD_CHUNK, SSD_CHUNK), SSD_CHUNK)
        dt_all = dts_ref[rows, :]
        dtt_ref[ci] = dt_all.T
        for direction in range(2):
            cum = _tri_cumsum(masks[direction][1], dt_all * a_neg)
            cumc_ref[direction, rows, :] = cum
            cumt_ref[direction, ci] = cum.T
        bt0_ref[ci] = bs0_ref[rows, :].astype(F32).T.astype(BF16)
        bt1_ref[ci] = bs1_ref[rows, :].astype(F32).T.astype(BF16)
        return carry

    lax.fori_loop(0, nchunks, prep, 0, unroll=2)

    def body(step, states):
        new_states = []
        for direction in range(2):
            reverse = direction == 1
            keep, tri = masks[direction]
            ci = _scan_chunk(step, nchunks, nctx, reverse)
            rows = pl.ds(pl.multiple_of(ci * SSD_CHUNK, SSD_CHUNK), SSD_CHUNK)
            x = xs_ref[rows, :]
            dt_all = dts_ref[rows, :]
            cum_all = cumc_ref[direction, rows, :]
            y = jnp.zeros((SSD_CHUNK, LANES), F32)
            for hh, (bt_ref, cs_ref) in enumerate(((bt0_ref, cs0_ref), (bt1_ref, cs1_ref))):
                chan = direction * SSD_HEADS + 2 * pair + hh
                sel = lane == chan
                dt_c = jnp.sum(jnp.where(sel, dt_all, 0.0), axis=-1, keepdims=True)
                cum_c = jnp.sum(jnp.where(sel, cum_all, 0.0), axis=-1, keepdims=True)
                dt_cb = jnp.broadcast_to(dt_c, (SSD_CHUNK, LANES))
                cum_cb = jnp.broadcast_to(cum_c, (SSD_CHUNK, LANES))
                cum_rb = cumt_ref[direction, ci, pl.ds(chan, 1), :]
                dt_rb = dtt_ref[ci, pl.ds(chan, 1), :]
                tot = cum_cb[0:1] if reverse else cum_cb[SSD_CHUNK - 1:SSD_CHUNK]
                seg = jnp.where(keep, jnp.exp(jnp.where(keep, cum_cb - cum_rb, 0.0)), 0.0)
                bt = bt_ref[ci]
                cm = cs_ref[rows, :]
                m = _dot(cm, bt) * seg * dt_rb
                half_mask = (lane // SSD_HD) == hh
                xm = jnp.where(half_mask, x, 0.0)
                st = states[2 * direction + hh]
                yh = _dot(m.astype(BF16), xm.astype(BF16)) + _dot(cm, st.astype(BF16)) * jnp.exp(cum_cb)
                w_end = dt_cb * jnp.exp(tot - cum_cb)
                st_new = st * jnp.exp(tot) + _dot(bt, (xm * w_end).astype(BF16))
                y = y + yh
                new_states.append(st_new)
            if direction == 0:
                o_ref[rows, :] = y + dskip_ref[...] * x
            else:
                yb_ref[rows, :] = y
        return tuple(new_states)

    zero = jnp.zeros((SSD_STATE, LANES), F32)
    lax.fori_loop(0, nchunks, body, (zero,) * 4, unroll=2)
    o_ref[...] += yb_ref[...]


def _ssd(y3, conv_w, conv_b, dt_bias_l, a_log_l, d_lanes, cos_t, sin_t, layer, ctx):
    b, lt, _ = y3.shape
    npair = SSD_HEADS // 2
    hg = SSD_HEADS // SSD_GROUPS
    xb = C_XBC // LANES
    bb = xb + SSD_WIDTH // LANES
    cb = bb + SSD_GROUPS
    g0 = lambda p: (2 * p) // hg
    g1 = lambda p: (2 * p + 1) // hg
    yblk = lambda f: pl.BlockSpec((None, lt, LANES), lambda bi, p, f=f: (bi, 0, f(p)))
    wblk = lambda f: pl.BlockSpec((None, SSD_CONV, LANES), lambda bi, p, f=f: (layer, 0, f(p)))
    bblk = lambda f: pl.BlockSpec((None, 1, LANES), lambda bi, p, f=f: (layer, 0, f(p)))
    cols = [lambda p: p, lambda p: npair + g0(p), lambda p: npair + g1(p),
            lambda p: npair + SSD_GROUPS + g0(p), lambda p: npair + SSD_GROUPS + g1(p)]
    const2 = lambda shape: pl.BlockSpec(shape, lambda bi, p: (0, 0))
    return pl.pallas_call(
        functools.partial(_ssd_kernel, ctx=ctx, lt=lt),
        out_shape=jax.ShapeDtypeStruct((b, lt, SSD_WIDTH), F32),
        grid=(b, npair),
        in_specs=[yblk(lambda p: xb + p), yblk(lambda p: bb + g0(p)), yblk(lambda p: bb + g1(p)),
                  yblk(lambda p: cb + g0(p)), yblk(lambda p: cb + g1(p)), yblk(lambda p: C_DT // LANES)]
                 + [wblk(f) for f in cols] + [bblk(f) for f in cols]
                 + [const2((1, LANES)), const2((1, LANES)),
                    pl.BlockSpec((None, 1, LANES), lambda bi, p: (layer, 0, p)),
                    const2(cos_t.shape), const2(sin_t.shape)],
        out_specs=pl.BlockSpec((None, lt, LANES), lambda bi, p: (bi, 0, p)),
        scratch_shapes=[pltpu.VMEM((lt, LANES), F32)] + [pltpu.VMEM((lt, LANES), BF16)] * 4
                       + [pltpu.VMEM((lt, LANES), F32)] * 2
                       + [pltpu.VMEM((2, lt, LANES), F32),
                          pltpu.VMEM((2, lt // SSD_CHUNK, SSD_CHUNK, LANES), F32),
                          pltpu.VMEM((lt // SSD_CHUNK, SSD_CHUNK, LANES), F32)]
                       + [pltpu.VMEM((lt // SSD_CHUNK, SSD_STATE, SSD_CHUNK), BF16)] * 2,
        compiler_params=_cparams(("arbitrary", "arbitrary"), 48),
        name="ssd",
    )(y3, y3, y3, y3, y3, y3, *([conv_w] * 5), *([conv_b] * 5), dt_bias_l, a_log_l, d_lanes, cos_t, sin_t)


def _mix_kernel(h_ref, yh_ref, yn_ref, ys_ref, z_ref, snw_ref, wh_ref, wn_ref, ws_ref, g0_ref, g1_ref, g2_ref,
                m_ref, ysn_ref):
    @pl.when(pl.program_id(1) == 0)
    def _():
        y = ys_ref[...] * _silu(z_ref[...])
        gw = SSD_WIDTH // SSD_GROUPS
        lane = lax.broadcasted_iota(jnp.int32, y.shape, 1)
        inv = jnp.zeros_like(y)
        for g in range(SSD_GROUPS):
            in_g = (lane >= g * gw) & (lane < (g + 1) * gw)
            ms = jnp.sum(jnp.where(in_g, y * y, 0.0), axis=-1, keepdims=True) * (1.0 / gw)
            inv = jnp.where(in_g, lax.rsqrt(ms + EPS), inv)
        ysn_ref[...] = (y * inv * snw_ref[...]).astype(BF16)

    h = h_ref[...]
    m = _sigmoid(_dot(h, g0_ref[...])) * _dot(yh_ref[...], wh_ref[...])
    m = m + _sigmoid(_dot(h, g1_ref[...])) * _dot(yn_ref[...], wn_ref[...])
    m = m + _sigmoid(_dot(h, g2_ref[...])) * _dot(ysn_ref[...], ws_ref[...])
    m_ref[...] = m.astype(m_ref.dtype)


def _mix(h, y_hg, y_na, y_ssd, y2d, ssd_nw, wb_hg, wb_na, wb_ssd, w_gate_bf, layer):
    r, d = h.shape
    tm, tn = 512, 512
    nt = d // tn
    row = lambda w: pl.BlockSpec((tm, w), lambda i, j: (i, 0))
    wspec = lambda k: pl.BlockSpec((None, k, tn), lambda i, j: (layer, 0, j))
    gspec = lambda br: pl.BlockSpec((None, d, tn), lambda i, j, br=br: (layer, 0, br * nt + j))
    return pl.pallas_call(
        _mix_kernel,
        out_shape=jax.ShapeDtypeStruct((r, d), BF16),
        grid=(r // tm, nt),
        in_specs=[row(d), row(HG_WIDTH), row(NA_WIDTH), row(SSD_WIDTH),
                  pl.BlockSpec((tm, SSD_WIDTH), lambda i, j: (i, C_Z // SSD_WIDTH)),
                  pl.BlockSpec((None, 1, SSD_WIDTH), lambda i, j: (layer, 0, 0)),
                  wspec(HG_WIDTH), wspec(NA_WIDTH), wspec(SSD_WIDTH), gspec(0), gspec(1), gspec(2)],
        out_specs=pl.BlockSpec((tm, tn), lambda i, j: (i, j)),
        scratch_shapes=[pltpu.VMEM((tm, SSD_WIDTH), BF16)],
        compiler_params=_cparams(("arbitrary", "arbitrary"), 48),
        name="mix",
    )(h, y_hg, y_na, y_ssd, y2d, ssd_nw, wb_hg, wb_na, wb_ssd, w_gate_bf, w_gate_bf, w_gate_bf)


def _outproj_kernel(x_ref, m_ref, mod_ref, nw_ref, wo_ref, wr_ref, br_ref, xo_ref, h2_ref, lg_ref,
                    *, d, chunks_per_seq, ctx_chunks, batch):
    row = _mod_row_index(pl.program_id(0), chunks_per_seq, ctx_chunks, batch)
    g1 = mod_ref[pl.ds(row, 1), 2 * d:3 * d]
    sh2 = mod_ref[pl.ds(row, 1), 3 * d:4 * d]
    sc2 = mod_ref[pl.ds(row, 1), 4 * d:5 * d]
    x = x_ref[...] + g1 * _dot(m_ref[...], wo_ref[...])
    xo_ref[...] = x
    h2 = _rms(x, nw_ref[...]) * (1.0 + sc2) + sh2
    h2_ref[...] = h2
    a, b, _ = _split3(h2)
    wa, wb, _ = _split3(wr_ref[...])
    lg_ref[...] = _dot(a, wa) + _dot(a, wb) + _dot(b, wa) + br_ref[...]


def _outproj(x2d, m, mod_l, norm_w, w_out_bf, w_router, b_router, layer, *, chunks_per_seq, ctx_chunks, batch):
    r, d = x2d.shape
    tm = MOD_CHUNK
    kern = functools.partial(_outproj_kernel, d=d, chunks_per_seq=chunks_per_seq, ctx_chunks=ctx_chunks,
                             batch=batch)
    return pl.pallas_call(
        kern,
        out_shape=(jax.ShapeDtypeStruct((r, d), F32), jax.ShapeDtypeStruct((r, d), F32),
                   jax.ShapeDtypeStruct((r, LANES), F32)),
        grid=(r // tm,),
        in_specs=[pl.BlockSpec((tm, d), lambda i: (i, 0)),
                  pl.BlockSpec((tm, d), lambda i: (i, 0)),
                  pl.BlockSpec(mod_l.shape, lambda i: (0, 0)),
                  pl.BlockSpec((None, 1, d), lambda i: (layer, 0, 0)),
                  pl.BlockSpec((None, d, d), lambda i: (layer, 0, 0)),
                  pl.BlockSpec((None, d, LANES), lambda i: (layer, 0, 0)),
                  pl.BlockSpec((None, 1, LANES), lambda i: (layer, 0, 0))],
        out_specs=(pl.BlockSpec((tm, d), lambda i: (i, 0)), pl.BlockSpec((tm, d), lambda i: (i, 0)),
                   pl.BlockSpec((tm, LANES), lambda i: (i, 0))),
        compiler_params=_cparams(("arbitrary",), 48),
        name="outproj",
    )(x2d, m, mod_l, norm_w, w_out_bf, w_router, b_router)


def _moe_kernel(ie_ref, ib_ref, in_ref, if_ref, x_hbm, wg_ref, wu_ref, wd_ref, bg_ref, bu_ref, bd_ref, y_hbm,
                xbuf, acc, sem_in, sem_out, *, nj, n_items):
    it, j = pl.program_id(0), pl.program_id(1)
    nblk = in_ref[it]
    nfill = if_ref[it]
    row0 = pl.multiple_of(ib_ref[it] * MOE_RB, MOE_RB)

    def in_copy(c):
        return pltpu.make_async_copy(x_hbm.at[pl.ds(row0 + c * MOE_RB, MOE_RB), :],
                                     xbuf.at[pl.ds(c * MOE_RB, MOE_RB), :], sem_in)

    def out_copy(c, src_block):
        return pltpu.make_async_copy(acc.at[pl.ds(src_block * MOE_RB, MOE_RB), :],
                                     y_hbm.at[pl.ds(row0 + c * MOE_RB, MOE_RB), :], sem_out)

    def wait_out(n):
        one = pltpu.make_async_copy(acc.at[pl.ds(0, MOE_RB), :], y_hbm.at[pl.ds(0, MOE_RB), :], sem_out)
        lax.fori_loop(0, n, lambda c, _: (one.wait(), 0)[1], 0)

    n_prev = jnp.where(it > 0, in_ref[jnp.maximum(it - 1, 0)], 0)

    @pl.when((it == 0) & (j == 0))
    def _():
        acc[...] = jnp.zeros(acc.shape, F32)

    @pl.when((j == 0) & (nblk == 0))
    def _():
        wait_out(n_prev)

    @pl.when((nfill > 0) & (j == 0))
    def _():
        acc[0:MOE_RB, :] = jnp.zeros((MOE_RB, acc.shape[1]), F32)
        lax.fori_loop(0, nfill, lambda c, _: (out_copy(c, 0).start(), 0)[1], 0)
        wait_out(nfill)

    @pl.when(nblk > 0)
    def _():
        @pl.when(j == 0)
        def _():
            lax.fori_loop(0, nblk, lambda c, _: (in_copy(c).start(), 0)[1], 0)
            lax.fori_loop(0, nblk, lambda c, _: (in_copy(c).wait(), 0)[1], 0)

        def block(c, weights):
            wg, wu, wd = weights
            start = c * MOE_RB if isinstance(c, int) else pl.multiple_of(c * MOE_RB, MOE_RB)
            rows = pl.ds(start, MOE_RB)
            xb = xbuf[rows, :].astype(BF16)
            gate = jnp.minimum(_dot(xb, wg) + bg_ref[...], SWIGLU_LIMIT)
            up = jnp.clip(_dot(xb, wu) + bu_ref[...], -SWIGLU_LIMIT, SWIGLU_LIMIT)
            a = gate * _sigmoid(SWIGLU_ALPHA * gate) * (up + 1.0)
            part = _dot(a.astype(BF16), wd)

            if not isinstance(c, int):
                pl.when((j == 0) & (c == 0))(lambda: wait_out(n_prev))
            elif c == 0:
                pl.when(j == 0)(lambda: wait_out(n_prev))

            base = jnp.where(j == 0, jnp.broadcast_to(bd_ref[...], part.shape), acc[rows, :])
            acc[rows, :] = base + part

        def cast_weights():
            return wg_ref[...].astype(BF16), wu_ref[...].astype(BF16), wd_ref[...].astype(BF16)

        @pl.when(nblk == MOE_KB)
        def _():
            weights = cast_weights()
            for c in range(MOE_KB):
                block(c, weights)

        @pl.when(nblk < MOE_KB)
        def _():
            weights = cast_weights()
            lax.fori_loop(0, nblk, lambda c, _: (block(c, weights), 0)[1], 0)

        @pl.when(j == nj - 1)
        def _():
            lax.fori_loop(0, nblk, lambda c, _: (out_copy(c, c).start(), 0)[1], 0)

            @pl.when(it == n_items - 1)
            def _():
                wait_out(nblk)


def _moe_experts(item_e, item_blk, item_n, item_fill, x_sorted, w_gate, b_gate, w_up, b_up, w_down, b_down, layer):
    cap, d = x_sorted.shape
    f = w_gate.shape[-1]
    nj = f // MOE_TF
    n_items = item_e.shape[0]
    depth, ne = b_gate.shape[:2]

    def jsel(it, j, n):
        return jnp.where(n[it] > 0, j, nj - 1)

    w_in = lambda: pl.BlockSpec((None, None, d, MOE_TF), lambda it, j, e, b, n, fl: (layer, e[it], 0, jsel(it, j, n)))
    b_in = lambda: pl.BlockSpec((None, None, 1, MOE_TF), lambda it, j, e, b, n, fl: (layer, e[it], 0, jsel(it, j, n)))
    grid_spec = pltpu.PrefetchScalarGridSpec(
        num_scalar_prefetch=4,
        grid=(n_items, nj),
        in_specs=[pl.BlockSpec(memory_space=pl.ANY), w_in(), w_in(),
                  pl.BlockSpec((None, None, MOE_TF, d), lambda it, j, e, b, n, fl: (layer, e[it], jsel(it, j, n), 0)),
                  b_in(), b_in(),
                  pl.BlockSpec((None, None, 1, d), lambda it, j, e, b, n, fl: (layer, e[it], 0, 0))],
        out_specs=pl.BlockSpec(memory_space=pl.ANY),
        scratch_shapes=[pltpu.VMEM((MOE_KB * MOE_RB, d), F32), pltpu.VMEM((MOE_KB * MOE_RB, d), F32),
                        pltpu.SemaphoreType.DMA, pltpu.SemaphoreType.DMA],
    )
    return pl.pallas_call(
        functools.partial(_moe_kernel, nj=nj, n_items=n_items),
        out_shape=jax.ShapeDtypeStruct((cap, d), F32),
        grid_spec=grid_spec,
        compiler_params=_cparams(("arbitrary", "arbitrary"), 60),
        name="moe_experts",
    )(item_e, item_blk, item_n, item_fill, x_sorted, w_gate, w_up, w_down,
      b_gate.reshape(depth, ne, 1, f), b_up.reshape(depth, ne, 1, f), b_down.reshape(depth, ne, 1, d))


def _latent_chunk(i, chunks_per_seq, ctx_chunks):
    lat = chunks_per_seq - ctx_chunks
    return (i // lat) * chunks_per_seq + ctx_chunks + i % lat


def _combine_kernel(x_ref, y_ref, w_ref, mod_ref, o_ref, *, d, chunks_per_seq, ctx_chunks, batch, latent_only):
    chunk = pl.program_id(0)
    if latent_only:
        chunk = _latent_chunk(chunk, chunks_per_seq, ctx_chunks)
    row = _mod_row_index(chunk, chunks_per_seq, ctx_chunks, batch)
    g2 = mod_ref[pl.ds(row, 1), 5 * d:6 * d]
    w = w_ref[...]
    f = y_ref[0] * w[:, 0:1]
    for k in range(1, TOP_K):
        f = f + y_ref[k] * w[:, k:k + 1]
    o_ref[...] = x_ref[...] + g2 * f


def _combine(x2d, yk, top_w, mod_l, latent_only, *, chunks_per_seq, ctx_chunks, batch):
    d = x2d.shape[1]
    r = yk.shape[1]
    tm = MOD_CHUNK
    kern = functools.partial(_combine_kernel, d=d, chunks_per_seq=chunks_per_seq, ctx_chunks=ctx_chunks,
                             batch=batch, latent_only=latent_only)
    x_chunk = (lambda i: (_latent_chunk(i, chunks_per_seq, ctx_chunks), 0)) if latent_only else (lambda i: (i, 0))
    return pl.pallas_call(
        kern,
        out_shape=jax.ShapeDtypeStruct((r, d), F32),
        grid=(r // tm,),
        in_specs=[pl.BlockSpec((tm, d), x_chunk),
                  pl.BlockSpec((TOP_K, tm, d), lambda i: (0, i, 0)),
                  pl.BlockSpec((tm, TOP_K), lambda i: (i, 0)),
                  pl.BlockSpec(mod_l.shape, lambda i: (0, 0))],
        out_specs=pl.BlockSpec((tm, d), lambda i: (i, 0)),
        compiler_params=_cparams(("arbitrary",), 40),
        name="combine",
    )(x2d, yk, top_w, mod_l)


def _moe_plan(logits, n_tok):
    top_val, top_idx = lax.top_k(logits, TOP_K)
    top_w = jax.nn.softmax(top_val, axis=-1)
    n_assign = n_tok * TOP_K
    exp_id = top_idx.reshape(-1)
    onehot = (exp_id[:, None] == jnp.arange(N_EXPERTS)[None, :]).astype(jnp.int32)
    csum = jnp.cumsum(onehot, axis=0)
    rank = jnp.sum((csum - onehot) * onehot, axis=1)
    counts = csum[-1]
    nblk = (counts + MOE_RB - 1) // MOE_RB
    blk_end = jnp.cumsum(nblk)
    blk_start = blk_end - nblk
    dest = blk_start[exp_id] * MOE_RB + rank
    total_blocks = (n_assign + N_EXPERTS * (MOE_RB - 1)) // MOE_RB
    cap = total_blocks * MOE_RB
    slot_tok = jnp.zeros((cap,), jnp.int32).at[dest].set(jnp.arange(n_assign, dtype=jnp.int32) // TOP_K)

    n_items = N_EXPERTS + total_blocks // MOE_KB
    items_e = (nblk + MOE_KB - 1) // MOE_KB
    item_end = jnp.cumsum(items_e)
    item_start = item_end - items_e
    idx = jnp.arange(n_items)
    e_of = jnp.minimum(jnp.searchsorted(item_end, idx, side="right"), N_EXPERTS - 1).astype(jnp.int32)
    valid = idx < item_end[-1]
    e_last = e_of[jnp.maximum(item_end[-1] - 1, 0)]
    kk = idx - item_start[e_of]
    item_e = jnp.where(valid, e_of, e_last).astype(jnp.int32)
    fill_blk = blk_end[-1] + (idx - item_end[-1]) * MOE_KB
    item_blk = jnp.where(valid, blk_start[e_of] + kk * MOE_KB, jnp.minimum(fill_blk, total_blocks)).astype(jnp.int32)
    item_n = jnp.where(valid, jnp.clip(nblk[e_of] - kk * MOE_KB, 0, MOE_KB), 0).astype(jnp.int32)
    item_fill = jnp.where(valid, 0, jnp.clip(total_blocks - fill_blk, 0, MOE_KB)).astype(jnp.int32)
    return top_w, dest.reshape(n_tok, TOP_K), slot_tok, item_e, item_blk, item_n, item_fill


def kernel(x, c, ctx, c_ctx, w_mod, b_mod, norm1_w, norm2_w, w_in, hg_lb_logits, hg_norm_w, na_q_norm_w,
           na_k_norm_w, na_rpb, ssd_conv_w, ssd_conv_b, ssd_dt_bias, ssd_a_log, ssd_d, ssd_norm_w, w_branch_hg,
           w_branch_na, w_branch_ssd, w_out, moe_w_router, moe_b_router, moe_w_gate, moe_b_gate, moe_w_up,
           moe_b_up, moe_w_down, moe_b_down):
    batch, seq, d = x.shape
    n_ctx = ctx.shape[1]
    depth = w_mod.shape[0]
    lt = n_ctx + seq
    r = batch * lt
    chunks = dict(chunks_per_seq=lt // MOD_CHUNK, ctx_chunks=n_ctx // MOD_CHUNK, batch=batch)
    assert n_ctx % MOD_CHUNK == 0 and seq % MOD_CHUNK == 0 and batch < SUBLANES

    lb_soft = jax.nn.softmax(hg_lb_logits.astype(F32), axis=0)
    lower_bounds = jnp.cumsum(lb_soft, axis=0) - lb_soft[0]
    w_proj_bf = w_in[:, :, :N_PROJ].astype(BF16)
    w_gate_bf = w_in[:, :, N_MIX_COLS:].astype(BF16)
    wb_hg, wb_na, wb_ssd = (w.astype(BF16) for w in (w_branch_hg, w_branch_na, w_branch_ssd))
    w_out_bf = w_out.astype(BF16)
    w_router = jnp.pad(moe_w_router, ((0, 0), (0, 0), (0, LANES - N_EXPERTS)))
    b_router = jnp.pad(moe_b_router, ((0, 0), (0, LANES - N_EXPERTS))).reshape(depth, 1, LANES)
    vec = lambda p: p.reshape(depth, 1, p.shape[-1])
    dt_pad = lambda p: jnp.pad(p.reshape(depth, 2 * SSD_HEADS), ((0, 0), (0, LANES - 2 * SSD_HEADS)))
    dt_bias_p, a_log_p = dt_pad(ssd_dt_bias), dt_pad(ssd_a_log)
    d_lanes = jnp.repeat(ssd_d, SSD_HD, axis=-1).reshape(depth, 1, SSD_WIDTH)
    cos_t, sin_t = _rope_tables(seq)
    conv_b = vec(ssd_conv_b)

    cvecs = jnp.zeros((SUBLANES, d), F32).at[:batch].set(c).at[batch].set(c_ctx)
    mod = _mod_table(cvecs, w_mod, b_mod)

    xa = jnp.concatenate([ctx, x], axis=1).reshape(r, d)
    for layer in range(depth):
        mod_l = mod[layer]
        y2d, h = _inproj(xa, mod_l, vec(norm1_w), w_proj_bf, layer, **chunks)
        y3 = y2d.reshape(batch, lt, N_PROJ)
        y_hg = _hgrn2(y3, lower_bounds[layer], vec(hg_norm_w), layer, n_ctx)
        bias_tab = _na_bias_table(na_rpb[layer], seq // GRID_W)
        y_na = _nattn(y3, bias_tab, vec(na_q_norm_w), vec(na_k_norm_w), layer, n_ctx)
        y_ssd = _ssd(y3, ssd_conv_w, conv_b, dt_bias_p[layer:layer + 1], a_log_p[layer:layer + 1], d_lanes,
                     cos_t, sin_t, layer, n_ctx)
        m = _mix(h, y_hg.reshape(r, -1), y_na.reshape(r, -1), y_ssd.reshape(r, -1), y2d, vec(ssd_norm_w),
                 wb_hg, wb_na, wb_ssd, w_gate_bf, layer)
        xa, h2, logits = _outproj(xa, m, mod_l, vec(norm2_w), w_out_bf, w_router, b_router, layer, **chunks)

        latent_only = layer == depth - 1
        if latent_only:
            logits = logits.reshape(batch, lt, LANES)[:, n_ctx:].reshape(batch * seq, LANES)
        n_tok = logits.shape[0]
        top_w, dest, slot_tok, item_e, item_blk, item_n, item_fill = _moe_plan(logits[:, :N_EXPERTS], n_tok)
        slot_row = (slot_tok // seq) * lt + n_ctx + slot_tok % seq if latent_only else slot_tok
        x_sorted = jnp.take(h2, slot_row, axis=0, mode="clip")
        y_sorted = _moe_experts(item_e, item_blk, item_n, item_fill, x_sorted, moe_w_gate, moe_b_gate, moe_w_up, moe_b_up,
                                moe_w_down, moe_b_down, layer)
        yk = jnp.take(y_sorted, dest.T, axis=0, mode="clip")
        xa = _combine(xa, yk, top_w, mod_l, latent_only, **chunks)
    return xa.reshape(batch, seq, d)
```
